```python
import jax, jax.numpy as jnp
from jax import lax
import numpy as np

D_MODEL = 2048
BATCH = 8
SEQ = 2048
DEPTH = 1

CHUNK = 64
Q_BLOCK = 128
ROPE_THETA = 500000.0
EPS = 1e-6
NEG_INF = -1e30
N_BRANCHES = 2

MLA_HEADS = 8
MLA_NOPE = 128
MLA_ROPE = 64
MLA_V = 128
MLA_Q_RANK = 512
MLA_KV_RANK = 256

DSA_HEADS = 8
DSA_KV_HEADS = 2
DSA_GROUP = DSA_HEADS // DSA_KV_HEADS
DSA_HEAD_DIM = 128
DSA_ROT = DSA_HEAD_DIM // 4
IDX_HEADS = 16
IDX_DIM = 64
IDX_ROT = IDX_DIM // 4
DSA_TOPK_MAX = 256

PEER_HEADS = 8
PEER_N_KEYS = 128
PEER_N_EXPERTS = PEER_N_KEYS * PEER_N_KEYS
PEER_KEY_DIM = 128
PEER_TOPK = 16
PEER_TOKEN_BLOCK = 128

IN_SIZES = (
    MLA_Q_RANK,
    MLA_KV_RANK,
    MLA_ROPE,
    DSA_HEADS * DSA_HEAD_DIM,
    DSA_KV_HEADS * DSA_HEAD_DIM,
    DSA_KV_HEADS * DSA_HEAD_DIM,
    IDX_HEADS * IDX_DIM,
    IDX_DIM,
    IDX_HEADS,
    N_BRANCHES * D_MODEL,
)
IN_WIDTH = sum(IN_SIZES)

kernel_name = "hybrid_mla_dsa_peer_chunk_causal"


def rmsnorm(t, g):
    tf = t.astype(jnp.float32)
    y = tf * lax.rsqrt(jnp.mean(tf * tf, axis=-1, keepdims=True) + EPS)
    return (y * g.astype(jnp.float32)).astype(t.dtype)


def rope(t, positions, rot_dim):
    half = rot_dim // 2
    inv_freq = ROPE_THETA ** (-(jnp.arange(half, dtype=jnp.float32) * 2.0) / rot_dim)
    ang = positions.astype(jnp.float32)[..., None] * inv_freq
    cos = jnp.cos(ang)[:, :, None, :]
    sin = jnp.sin(ang)[:, :, None, :]
    t1 = t[..., :half].astype(jnp.float32)
    t2 = t[..., half:rot_dim].astype(jnp.float32)
    rest = t[..., rot_dim:]
    return jnp.concatenate([(t1 * cos - t2 * sin).astype(t.dtype),
                            (t2 * cos + t1 * sin).astype(t.dtype), rest], axis=-1)


def to_blocks(t):
    B, S = t.shape[:2]
    return t.reshape((B, S // Q_BLOCK, Q_BLOCK) + t.shape[2:]).swapaxes(0, 1)


def from_blocks(t):
    nb, B, qb, w = t.shape
    return t.swapaxes(0, 1).reshape(B, nb * qb, w)


def chunk_causal_attention(q, k, v, scale):
    S = q.shape[1]
    n_blocks = S // Q_BLOCK
    starts = jnp.arange(n_blocks, dtype=jnp.int32) * Q_BLOCK
    key_chunk = jnp.arange(S, dtype=jnp.int32) // CHUNK

    def one_block(args):
        q_blk, q0 = args
        q_chunk = (q0 + jnp.arange(Q_BLOCK, dtype=jnp.int32)) // CHUNK
        allowed = key_chunk[None, :] <= q_chunk[:, None]
        s = jnp.einsum('bqhd,bkhd->bhqk', q_blk, k).astype(jnp.float32) * scale
        s = jnp.where(allowed[None, None], s, NEG_INF)
        p = jax.nn.softmax(s, axis=-1).astype(v.dtype)
        o = jnp.einsum('bhqk,bkhd->bqhd', p, v)
        return o.reshape(o.shape[0], Q_BLOCK, -1)

    return from_blocks(lax.map(one_block, (to_blocks(q), starts)))


def mla_branch(c_q, c_kv, k_rope, positions, q_norm_g, kv_norm_g, w_uq, w_uk, w_uv):
    B, S, _ = c_q.shape
    q = jnp.einsum('bsr,rhd->bshd', rmsnorm(c_q, q_norm_g), w_uq)
    q = jnp.concatenate([q[..., :MLA_NOPE], rope(q[..., MLA_NOPE:], positions, MLA_ROPE)], axis=-1)
    ckv = rmsnorm(c_kv, kv_norm_g)
    k_nope = jnp.einsum('bsr,rhd->bshd', ckv, w_uk)
    v = jnp.einsum('bsr,rhd->bshd', ckv, w_uv)
    k_pe = rope(k_rope[:, :, None, :], positions, MLA_ROPE)
    k = jnp.concatenate([k_nope, jnp.broadcast_to(k_pe, (B, S, MLA_HEADS, MLA_ROPE))], axis=-1)
    return chunk_causal_attention(q, k, v, (MLA_NOPE + MLA_ROPE) ** -0.5)


def dsa_branch(q, k, v, q_idx, k_idx, w_idx, positions):
    B, S, _ = q.shape
    q = rope(q.reshape(B, S, DSA_HEADS, DSA_HEAD_DIM), positions, DSA_ROT)
    k = rope(k.reshape(B, S, DSA_KV_HEADS, DSA_HEAD_DIM), positions, DSA_ROT)
    v = v.reshape(B, S, DSA_KV_HEADS, DSA_HEAD_DIM)
    q_idx = rope(q_idx.reshape(B, S, IDX_HEADS, IDX_DIM), positions, IDX_ROT)
    k_idx = rope(k_idx[:, :, None, :], positions, IDX_ROT)[:, :, 0, :]
    top_k = min(DSA_TOPK_MAX, S // 4)
    n_blocks = S // Q_BLOCK
    starts = jnp.arange(n_blocks, dtype=jnp.int32) * Q_BLOCK
    key_chunk = jnp.arange(S, dtype=jnp.int32) // CHUNK
    scale = DSA_HEAD_DIM ** -0.5
    gather_rows = jax.vmap(lambda table, idx: table[idx])

    def one_block(args):
        q_blk, qi_blk, w_blk, q0 = args
        q_chunk = (q0 + jnp.arange(Q_BLOCK, dtype=jnp.int32)) // CHUNK
        allowed = key_chunk[None, :] <= q_chunk[:, None]
        dots = jnp.einsum('bqhd,bsd->bqhs', qi_blk, k_idx).astype(jnp.float32) * IDX_DIM ** -0.5
        w = w_blk.astype(jnp.float32) * IDX_HEADS ** -0.5
        index_score = jnp.einsum('bqh,bqhs->bqs', w, jax.nn.relu(dots))
        index_score = jnp.where(allowed[None], index_score, NEG_INF)
        _, sel = lax.top_k(index_score, top_k)
        valid = (sel // CHUNK) <= q_chunk[None, :, None]
        k_sel = gather_rows(k, sel)
        v_sel = gather_rows(v, sel)
        qg = q_blk.reshape(B, Q_BLOCK, DSA_KV_HEADS, DSA_GROUP, DSA_HEAD_DIM)
        s = jnp.einsum('bqgnd,bqkgd->bqgnk', qg, k_sel).astype(jnp.float32) * scale
        s = jnp.where(valid[:, :, None, None, :], s, NEG_INF)
        p = jax.nn.softmax(s, axis=-1).astype(v.dtype)
        o = jnp.einsum('bqgnk,bqkgd->bqgnd', p, v_sel)
        return o.reshape(B, Q_BLOCK, DSA_HEADS * DSA_HEAD_DIM)

    out = lax.map(one_block, (to_blocks(q), to_blocks(q_idx), to_blocks(w_idx), starts))
    return from_blocks(out)


def peer_ffn(h, w_q, sub_keys, u_emb, v_emb):
    B, S, D = h.shape
    n_blocks = (B * S) // PEER_TOKEN_BLOCK
    half = PEER_KEY_DIM // 2

    def one_block(xb):
        q = jnp.einsum('td,dhk->thk', xb, w_q).reshape(PEER_TOKEN_BLOCK, PEER_HEADS, 2, half)
        s = jnp.einsum('thpc,hpnc->thpn', q, sub_keys).astype(jnp.float32)
        s1, i1 = lax.top_k(s[:, :, 0], PEER_TOPK)
        s2, i2 = lax.top_k(s[:, :, 1], PEER_TOPK)
        cand_s = (s1[..., :, None] + s2[..., None, :]).reshape(PEER_TOKEN_BLOCK, PEER_HEADS, -1)
        cand_i = (i1[..., :, None] * PEER_N_KEYS + i2[..., None, :]).reshape(PEER_TOKEN_BLOCK, PEER_HEADS, -1)
        top_s, pos = lax.top_k(cand_s, PEER_TOPK)
        expert = jnp.take_along_axis(cand_i, pos, axis=-1)
        g = jax.nn.softmax(top_s, axis=-1)
        u = u_emb[expert]
        a = jax.nn.gelu(jnp.einsum('thkd,td->thk', u, xb).astype(jnp.float32), approximate=False)
        coef = (g * a).astype(xb.dtype)
        return jnp.einsum('thk,thkd->td', coef, v_emb[expert])

    out = lax.map(one_block, h.reshape(n_blocks, PEER_TOKEN_BLOCK, D))
    return out.reshape(B, S, D)


def setup_inputs(seed: int = 0) -> dict:
    key = jax.random.key(seed)
    ks = jax.random.split(key, 20)
    f32 = jnp.float32
    L = DEPTH

    def nrm(k, shape, fan_in):
        return jax.random.normal(k, shape, f32) * (fan_in ** -0.5)

    def gain(k, shape):
        return 1.0 + 0.01 * jax.random.normal(k, shape, f32)

    x = jax.random.normal(ks[0], (BATCH, SEQ, D_MODEL), f32)
    start = jax.random.randint(ks[1], (BATCH, 1), 0, 64, dtype=jnp.int32) * CHUNK
    positions = start + jnp.arange(SEQ, dtype=jnp.int32)[None, :]
    return {
        'x': x,
        'positions': positions,
        'norm_mix_g': gain(ks[2], (L, D_MODEL)),
        'w_in': nrm(ks[3], (L, D_MODEL, IN_WIDTH), D_MODEL),
        'mla_q_norm_g': gain(ks[4], (L, MLA_Q_RANK)),
        'mla_kv_norm_g': gain(ks[5], (L, MLA_KV_RANK)),
        'mla_w_uq': nrm(ks[6], (L, MLA_Q_RANK, MLA_HEADS, MLA_NOPE + MLA_ROPE), MLA_Q_RANK),
        'mla_w_uk': nrm(ks[7], (L, MLA_KV_RANK, MLA_HEADS, MLA_NOPE), MLA_KV_RANK),
        'mla_w_uv': nrm(ks[8], (L, MLA_KV_RANK, MLA_HEADS, MLA_V), MLA_KV_RANK),
        'w_branch_a': nrm(ks[9], (L, MLA_HEADS * MLA_V, D_MODEL), MLA_HEADS * MLA_V),
        'w_branch_b': nrm(ks[10], (L, DSA_HEADS * DSA_HEAD_DIM, D_MODEL), DSA_HEADS * DSA_HEAD_DIM),
        'w_out': nrm(ks[11], (L, D_MODEL, D_MODEL), D_MODEL),
        'norm_ffn_g': gain(ks[12], (L, D_MODEL)),
        'peer_w_q': nrm(ks[13], (L, D_MODEL, PEER_HEADS, PEER_KEY_DIM), D_MODEL),
        'peer_sub_keys': nrm(ks[14], (L, PEER_HEADS, 2, PEER_N_KEYS, PEER_KEY_DIM // 2), PEER_KEY_DIM // 2),
        'peer_u': nrm(ks[15], (L, PEER_N_EXPERTS, D_MODEL), D_MODEL),
        'peer_v': nrm(ks[16], (L, PEER_N_EXPERTS, D_MODEL), PEER_HEADS),
        'norm_final_g': gain(ks[17], (D_MODEL,)),
    }


def reference(x, positions, norm_mix_g, w_in, mla_q_norm_g, mla_kv_norm_g, mla_w_uq, mla_w_uk,
              mla_w_uv, w_branch_a, w_branch_b, w_out, norm_ffn_g, peer_w_q, peer_sub_keys,
              peer_u, peer_v, norm_final_g):
    B, S, _ = x.shape
    splits = [int(c) for c in np.cumsum(IN_SIZES)[:-1]]
    for l in range(DEPTH):
        h = rmsnorm(x, norm_mix_g[l])
        proj = jnp.einsum('bsd,de->bse', h, w_in[l])
        c_q, c_kv, k_rope, dq, dk, dv, iq, ik, iw, gates = jnp.split(proj, splits, axis=-1)
        o_a = mla_branch(c_q, c_kv, k_rope, positions, mla_q_norm_g[l], mla_kv_norm_g[l],
                         mla_w_uq[l], mla_w_uk[l], mla_w_uv[l])
        o_b = dsa_branch(dq, dk, dv, iq, ik, iw, positions)
        gate = jax.nn.sigmoid(gates.astype(jnp.float32)).astype(x.dtype).reshape(B, S, N_BRANCHES, D_MODEL)
        merged = gate[:, :, 0, :] * (o_a @ w_branch_a[l]) + gate[:, :, 1, :] * (o_b @ w_branch_b[l])
        x = x + merged @ w_out[l]
        x = x + peer_ffn(rmsnorm(x, norm_ffn_g[l]), peer_w_q[l], peer_sub_keys[l], peer_u[l], peer_v[l])
    return rmsnorm(x, norm_final_g)
```

```python
import functools

import numpy as np
import jax
import jax.numpy as jnp
from jax import lax
from jax.experimental import pallas as pl
from jax.experimental.pallas import tpu as pltpu

F32 = jnp.float32
BF16 = jnp.bfloat16
I32 = jnp.int32

D_MODEL = 2048
CHUNK = 64
Q_BLOCK = 128
ROPE_THETA = 500000.0
EPS = 1e-6
NEG_INF = -1e30

MLA_HEADS = 8
MLA_NOPE = 128
MLA_ROPE = 64
MLA_V = 128
MLA_Q_RANK = 512
MLA_KV_RANK = 256
MLA_QK_PAD = 256

DSA_HEADS = 8
DSA_KV_HEADS = 2
DSA_GROUP = DSA_HEADS // DSA_KV_HEADS
DSA_HEAD_DIM = 128
DSA_ROT = DSA_HEAD_DIM // 4
IDX_HEADS = 16
IDX_DIM = 64
IDX_ROT = IDX_DIM // 4
DSA_TOPK_MAX = 256

PEER_HEADS = 8
PEER_N_KEYS = 128
PEER_KEY_DIM = 128
PEER_TOPK = 16

IN_SIZES = (MLA_Q_RANK, MLA_KV_RANK, MLA_ROPE, DSA_HEADS * DSA_HEAD_DIM, DSA_KV_HEADS * DSA_HEAD_DIM,
            DSA_KV_HEADS * DSA_HEAD_DIM, IDX_HEADS * IDX_DIM, IDX_DIM, IDX_HEADS, 2 * D_MODEL)

LANES = 128

OFF_GATES = 0
OFF_DQ = 4096
OFF_IQ = 5120
OFF_CQ = 6144
OFF_CKV = 6656
OFF_DK = 6912
OFF_DV = 7168
OFF_KROPE = 7424
OFF_IKW = 7552
N_PACK = 7680

VMEM_LIMIT = 56 * 1024 * 1024


def _params(*sem):
    return pltpu.CompilerParams(dimension_semantics=sem, vmem_limit_bytes=VMEM_LIMIT)


def _rms(t, g):
    return t * lax.rsqrt(jnp.mean(t * t, axis=-1, keepdims=True) + EPS) * g


def _dot(a, b):
    return jnp.dot(a, b, preferred_element_type=F32)


def _dot_nt(a, b):
    return lax.dot_general(a, b, (((1,), (1,)), ((), ())), preferred_element_type=F32)


def _rope(t, c, sa, sb, half):
    return t * c + pltpu.roll(t, half, 1) * sa + pltpu.roll(t, LANES - half, 1) * sb


def _inproj_kernel(x_ref, g_ref, w_ref, o_ref, h_ref):
    @pl.when(pl.program_id(1) == 0)
    def _():
        h_ref[...] = _rms(x_ref[...], g_ref[...]).astype(BF16)

    o_ref[...] = _dot(h_ref[...], w_ref[...])


def _inproj(x, g, w, tm=512, tn=1920):
    T = x.shape[0]
    return pl.pallas_call(
        _inproj_kernel,
        grid=(T // tm, N_PACK // tn),
        in_specs=[pl.BlockSpec((tm, D_MODEL), lambda i, j: (i, 0)),
                  pl.BlockSpec((1, D_MODEL), lambda i, j: (0, 0)),
                  pl.BlockSpec((D_MODEL, tn), lambda i, j: (0, j))],
        out_specs=pl.BlockSpec((tm, tn), lambda i, j: (i, j)),
        out_shape=jax.ShapeDtypeStruct((T, N_PACK), F32),
        scratch_shapes=[pltpu.VMEM((tm, D_MODEL), BF16)],
        compiler_params=_params("parallel", "arbitrary"),
        name="inproj",
    )(x, g, w)


def _prep_kernel(cq_ref, ckv_ref, kr_ref, dq_ref, dk_ref, dv_ref, iq_ref, ikw_ref,
                 tm_ref, td_ref, ti_ref, gq_ref, gkv_ref, wuq_ref, wuk_ref, wuv_ref,
                 qm_ref, km_ref, vm_ref, dqo_ref, dko_ref, dvo_ref, iqo_ref, ka_ref, kb_ref):
    cm, sam, sbm = tm_ref[0], tm_ref[1], tm_ref[2]
    cd, sad, sbd = td_ref[0], td_ref[1], td_ref[2]
    ci, sai, sbi = ti_ref[0], ti_ref[1], ti_ref[2]

    qn = _rms(cq_ref[...], gq_ref[...]).astype(BF16)
    q = _dot(qn, wuq_ref[...])
    for h in range(MLA_HEADS):
        lo = h * MLA_QK_PAD
        qm_ref[:, lo:lo + LANES] = q[:, lo:lo + LANES].astype(BF16)
        qm_ref[:, lo + LANES:lo + 2 * LANES] = _rope(q[:, lo + LANES:lo + 2 * LANES], cm, sam, sbm,
                                                       MLA_ROPE // 2).astype(BF16)
    ckv = _rms(ckv_ref[...], gkv_ref[...]).astype(BF16)
    kn = _dot(ckv, wuk_ref[...])
    kpe = _rope(kr_ref[...], cm, sam, sbm, MLA_ROPE // 2).astype(BF16)
    for h in range(MLA_HEADS):
        lo = h * MLA_QK_PAD
        km_ref[:, lo:lo + LANES] = kn[:, h * MLA_NOPE:(h + 1) * MLA_NOPE].astype(BF16)
        km_ref[:, lo + LANES:lo + 2 * LANES] = kpe
    vm_ref[...] = _dot(ckv, wuv_ref[...]).astype(BF16)

    for h in range(DSA_HEADS):
        sl = slice(h * LANES, (h + 1) * LANES)
        dqo_ref[:, sl] = _rope(dq_ref[:, sl], cd, sad, sbd, DSA_ROT // 2).astype(BF16)
    for h in range(DSA_KV_HEADS):
        sl = slice(h * LANES, (h + 1) * LANES)
        dko_ref[:, sl] = _rope(dk_ref[:, sl], cd, sad, sbd, DSA_ROT // 2).astype(BF16)
    dvo_ref[...] = dv_ref[...].astype(BF16)

    for c in range(IDX_HEADS * IDX_DIM // LANES):
        sl = slice(c * LANES, (c + 1) * LANES)
        iqo_ref[:, sl] = _rope(iq_ref[:, sl], ci, sai, sbi, IDX_ROT // 2).astype(BF16)
    lane = lax.broadcasted_iota(I32, ikw_ref.shape, 1)
    ka = jnp.where(lane < IDX_DIM, _rope(ikw_ref[...], ci, sai, sbi, IDX_ROT // 2), 0.0)
    ka_ref[...] = ka.astype(BF16)
    kb_ref[...] = pltpu.roll(ka, IDX_DIM, 1).astype(BF16)


def _prep(proj, tab_m, tab_d, tab_i, gq, gkv, wuq, wuk, wuv, tm=256):
    T = proj.shape[0]

    def col(width, off):
        return pl.BlockSpec((tm, width), lambda i: (i, off // width))

    def full(a):
        return pl.BlockSpec(a.shape, lambda i: (0,) * a.ndim)

    tab = pl.BlockSpec((3, tm, LANES), lambda i: (0, i, 0))
    widths = (MLA_HEADS * MLA_QK_PAD, MLA_HEADS * MLA_QK_PAD, MLA_HEADS * MLA_V, DSA_HEADS * DSA_HEAD_DIM,
              DSA_KV_HEADS * DSA_HEAD_DIM, DSA_KV_HEADS * DSA_HEAD_DIM, IDX_HEADS * IDX_DIM, LANES, LANES)
    return pl.pallas_call(
        _prep_kernel,
        grid=(T // tm,),
        in_specs=[col(MLA_Q_RANK, OFF_CQ), col(MLA_KV_RANK, OFF_CKV), col(LANES, OFF_KROPE),
                  col(1024, OFF_DQ), col(256, OFF_DK), col(256, OFF_DV), col(1024, OFF_IQ), col(LANES, OFF_IKW),
                  tab, tab, tab, full(gq), full(gkv), full(wuq), full(wuk), full(wuv)],
        out_specs=[pl.BlockSpec((tm, w), lambda i: (i, 0)) for w in widths],
        out_shape=[jax.ShapeDtypeStruct((T, w), BF16) for w in widths],
        compiler_params=_params("parallel"),
        name="prep",
    )(proj, proj, proj, proj, proj, proj, proj, proj, tab_m, tab_d, tab_i, gq, gkv, wuq, wuk, wuv)


def _mla_attn_kernel(q_ref, k_ref, v_ref, o_ref, *, tq, scale):
    S = k_ref.shape[0]
    q0 = pl.program_id(1) * tq
    q_chunk = (q0 + lax.broadcasted_iota(I32, (tq, S), 0)) // CHUNK
    k_chunk = lax.broadcasted_iota(I32, (tq, S), 1) // CHUNK
    allowed = k_chunk <= q_chunk
    for h in range(MLA_HEADS):
        qk = slice(h * MLA_QK_PAD, (h + 1) * MLA_QK_PAD)
        s = _dot_nt(q_ref[:, qk], k_ref[:, qk]) * scale
        s = jnp.where(allowed, s, NEG_INF)
        p = jnp.exp(s - jnp.max(s, axis=-1, keepdims=True))
        l = jnp.sum(p, axis=-1, keepdims=True)
        vs = slice(h * MLA_V, (h + 1) * MLA_V)
        o_ref[:, vs] = (_dot(p.astype(BF16), v_ref[:, vs]) / l).astype(BF16)


def _mla_attn(q, k, v, B, S, tq=256):
    nq = S // tq
    return pl.pallas_call(
        functools.partial(_mla_attn_kernel, tq=tq, scale=(MLA_NOPE + MLA_ROPE) ** -0.5),
        grid=(B, nq),
        in_specs=[pl.BlockSpec((tq, q.shape[1]), lambda b, i: (b * nq + i, 0)),
                  pl.BlockSpec((S, k.shape[1]), lambda b, i: (b, 0)),
                  pl.BlockSpec((S, v.shape[1]), lambda b, i: (b, 0))],
        out_specs=pl.BlockSpec((tq, v.shape[1]), lambda b, i: (b * nq + i, 0)),
        out_shape=jax.ShapeDtypeStruct((B * S, v.shape[1]), BF16),
        compiler_params=_params("parallel", "arbitrary"),
        name="mla_attn",
    )(q, k, v)


def _order_key(score):
    bits = pltpu.bitcast(score + 0.0, I32)
    return bits ^ ((bits >> 31) & jnp.int32(0x7FFFFFFF))


def _dsa_kernel(iq_ref, ikw_ref, dq_ref, ka_ref, kb_ref, dk_ref, dv_ref, o_ref, *, top_k, scale):
    S = ka_ref.shape[0]
    tq = Q_BLOCK
    q0 = pl.program_id(1) * tq
    q_chunk = (q0 + lax.broadcasted_iota(I32, (tq, S), 0)) // CHUNK
    k_chunk = lax.broadcasted_iota(I32, (tq, S), 1) // CHUNK
    allowed = k_chunk <= q_chunk

    w = ikw_ref[...] * (IDX_HEADS ** -0.5)
    score = jnp.zeros((tq, S), F32)
    for c in range(IDX_HEADS // 2):
        qpair = iq_ref[:, c * LANES:(c + 1) * LANES]
        for half, k_ref in enumerate((ka_ref, kb_ref)):
            h = 2 * c + half
            dots = _dot_nt(qpair, k_ref[...]) * (IDX_DIM ** -0.5)
            score = score + w[:, IDX_DIM + h:IDX_DIM + h + 1] * jnp.maximum(dots, 0.0)
    score = jnp.where(allowed, score, NEG_INF)

    keys = _order_key(score)
    kf = jnp.float32(top_k)

    def count_ge(cand):
        return jnp.sum((keys >= cand).astype(F32), axis=-1, keepdims=True)

    int_min = jnp.full((tq, 1), jnp.iinfo(jnp.int32).min, I32)
    thr = jnp.where(count_ge(jnp.zeros((tq, 1), I32)) >= kf, 0, int_min)

    def step(i, thr):
        cand = thr + jnp.left_shift(jnp.int32(1), 30 - i)
        return jnp.where(count_ge(cand) >= kf, cand, thr)

    thr = lax.fori_loop(0, 31, step, thr)

    gt = keys > thr
    eq = keys == thr
    need = kf - jnp.sum(gt.astype(F32), axis=-1, keepdims=True)
    tri = (lax.broadcasted_iota(I32, (LANES, LANES), 0) <= lax.broadcasted_iota(I32, (LANES, LANES), 1)).astype(BF16)
    run = jnp.zeros((tq, 1), F32)
    sel = []
    for c in range(S // LANES):
        sl = slice(c * LANES, (c + 1) * LANES)
        prefix = _dot(eq[:, sl].astype(BF16), tri) + run
        sel.append(gt[:, sl] | (eq[:, sl] & (prefix <= need)))
        run = prefix[:, LANES - 1:LANES]
    mask = jnp.concatenate(sel, axis=1) & allowed

    mask_g = jnp.concatenate([mask] * DSA_GROUP, axis=0)
    for g in range(DSA_KV_HEADS):
        kv = slice(g * DSA_HEAD_DIM, (g + 1) * DSA_HEAD_DIM)
        qg = jnp.concatenate([dq_ref[:, (g * DSA_GROUP + n) * DSA_HEAD_DIM:(g * DSA_GROUP + n + 1) * DSA_HEAD_DIM]
                              for n in range(DSA_GROUP)], axis=0)
        s = _dot_nt(qg, dk_ref[:, kv]) * scale
        s = jnp.where(mask_g, s, NEG_INF)
        p = jnp.exp(s - jnp.max(s, axis=-1, keepdims=True))
        l = jnp.sum(p, axis=-1, keepdims=True)
        o = _dot(p.astype(BF16), dv_ref[:, kv]) / l
        for n in range(DSA_GROUP):
            hs = (g * DSA_GROUP + n) * DSA_HEAD_DIM
            o_ref[:, hs:hs + DSA_HEAD_DIM] = o[n * tq:(n + 1) * tq].astype(BF16)


def _dsa(iq, proj, dq, ka, kb, dk, dv, B, S):
    tq = Q_BLOCK
    nq = S // tq

    def qblk(width, off=0):
        return pl.BlockSpec((tq, width), lambda b, i: (b * nq + i, off // width))

    def kblk(width):
        return pl.BlockSpec((S, width), lambda b, i: (b, 0))

    return pl.pallas_call(
        functools.partial(_dsa_kernel, top_k=min(DSA_TOPK_MAX, S // 4), scale=DSA_HEAD_DIM ** -0.5),
        grid=(B, nq),
        in_specs=[qblk(1024), qblk(LANES, OFF_IKW), qblk(1024), kblk(LANES), kblk(LANES), kblk(256), kblk(256)],
        out_specs=qblk(1024),
        out_shape=jax.ShapeDtypeStruct((B * S, DSA_HEADS * DSA_HEAD_DIM), BF16),
        compiler_params=_params("parallel", "arbitrary"),
        name="dsa",
    )(iq, proj, dq, ka, kb, dk, dv)


def _merge_kernel(oa_ref, ob_ref, ga_ref, gb_ref, wa_ref, wb_ref, o_ref):
    a = _dot(oa_ref[...], wa_ref[...])
    b = _dot(ob_ref[...], wb_ref[...])
    o_ref[...] = (jax.nn.sigmoid(ga_ref[...]) * a + jax.nn.sigmoid(gb_ref[...]) * b).astype(BF16)


def _merge(oa, ob, proj, wa, wb, tm=256):
    T = oa.shape[0]
    row = lambda w, c=0: pl.BlockSpec((tm, w), lambda i: (i, c))
    full = lambda a: pl.BlockSpec(a.shape, lambda i: (0, 0))
    return pl.pallas_call(
        _merge_kernel,
        grid=(T // tm,),
        in_specs=[row(1024), row(1024), row(D_MODEL, 0), row(D_MODEL, 1), full(wa), full(wb)],
        out_specs=row(D_MODEL),
        out_shape=jax.ShapeDtypeStruct((T, D_MODEL), BF16),
        compiler_params=_params("parallel"),
        name="merge",
    )(oa, ob, proj, proj, wa, wb)


def _outproj_kernel(x_ref, m_ref, wo_ref, g_ref, wq_ref, x1_ref, h2_ref, qp_ref):
    x1 = x_ref[...] + _dot(m_ref[...], wo_ref[...])
    x1_ref[...] = x1
    h2 = _rms(x1, g_ref[...]).astype(BF16)
    h2_ref[...] = h2
    qp_ref[...] = _dot(h2, wq_ref[...]).astype(BF16)


def _outproj(x, merged, wo, g, wq, tm=256):
    T = x.shape[0]
    row = lambda w: pl.BlockSpec((tm, w), lambda i: (i, 0))
    full = lambda a: pl.BlockSpec(a.shape, lambda i: (0, 0))
    nq = wq.shape[1]
    return pl.pallas_call(
        _outproj_kernel,
        grid=(T // tm,),
        in_specs=[row(D_MODEL), row(D_MODEL), full(wo), full(g), full(wq)],
        out_specs=[row(D_MODEL), row(D_MODEL), row(nq)],
        out_shape=[jax.ShapeDtypeStruct((T, D_MODEL), F32), jax.ShapeDtypeStruct((T, D_MODEL), BF16),
                   jax.ShapeDtypeStruct((T, nq), BF16)],
        compiler_params=_params("parallel"),
        name="outproj",
    )(x, merged, wo, g, wq)


def _top16(cur, payload=None):
    n = cur.shape[0]
    iota = lax.broadcasted_iota(I32, cur.shape, 0)
    vals, idxs = [], []
    for _ in range(PEER_TOPK):
        m = jnp.max(cur, axis=0, keepdims=True)
        idx = jnp.min(jnp.where(cur == m, iota, n), axis=0, keepdims=True)
        hit = iota == idx
        vals.append(m)
        if payload is None:
            idxs.append(idx)
        else:
            idxs.append(jnp.max(jnp.where(hit, payload, -1), axis=0, keepdims=True))
        cur = jnp.where(hit, -jnp.inf, cur)
    return jnp.concatenate(vals, axis=0), jnp.concatenate(idxs, axis=0)


def _route_kernel(qp_ref, keys_ref, idx_ref, g_ref):
    for h in range(PEER_HEADS):
        qpair = qp_ref[:, h * PEER_KEY_DIM:(h + 1) * PEER_KEY_DIM]
        s1, i1 = _top16(_dot_nt(keys_ref[2 * h], qpair))
        s2, i2 = _top16(_dot_nt(keys_ref[2 * h + 1], qpair))
        cand_s = jnp.concatenate([s1[a:a + 1] + s2 for a in range(PEER_TOPK)], axis=0)
        cand_i = jnp.concatenate([i1[a:a + 1] * PEER_N_KEYS + i2 for a in range(PEER_TOPK)], axis=0)
        top_s, expert = _top16(cand_s, cand_i)
        e = jnp.exp(top_s - top_s[0:1])
        g = e / jnp.sum(e, axis=0, keepdims=True)
        idx_ref[0, h * PEER_TOPK:(h + 1) * PEER_TOPK, :] = expert
        g_ref[0, h * PEER_TOPK:(h + 1) * PEER_TOPK, :] = g


def _route(qp, keys_pad, tb=128):
    T = qp.shape[0]
    nb = T // tb
    nk = PEER_HEADS * PEER_TOPK
    out = pl.BlockSpec((1, nk, tb), lambda i: (i, 0, 0))
    return pl.pallas_call(
        _route_kernel,
        grid=(nb,),
        in_specs=[pl.BlockSpec((tb, qp.shape[1]), lambda i: (i, 0)),
                  pl.BlockSpec(keys_pad.shape, lambda i: (0, 0, 0))],
        out_specs=[out, out],
        out_shape=[jax.ShapeDtypeStruct((nb, nk, tb), I32), jax.ShapeDtypeStruct((nb, nk, tb), F32)],
        compiler_params=_params("parallel"),
        name="peer_route",
    )(qp, keys_pad)


PEER_TG = 8


def _peer_kernel(idx_ref, g_ref, h_ref, x1_ref, gf_ref, u_hbm, v_hbm, y_ref, ubuf, vbuf, sem):
    nk = PEER_HEADS * PEER_TOPK
    rows = PEER_TG * nk

    def copies(r, e):
        return (pltpu.make_async_copy(u_hbm.at[pl.ds(e, 1)], ubuf.at[pl.ds(r, 1)], sem.at[0]),
                pltpu.make_async_copy(v_hbm.at[pl.ds(e, 1)], vbuf.at[pl.ds(r, 1)], sem.at[1]))

    for t in range(PEER_TG):
        def issue(k, carry):
            cu, cv = copies(t * nk + k, idx_ref[t, k])
            cu.start()
            cv.start()
            return carry
        lax.fori_loop(0, nk, issue, 0)

    def drain(r, carry):
        cu, cv = copies(r, 0)
        cu.wait()
        cv.wait()
        return carry
    lax.fori_loop(0, rows, drain, 0)

    res = _dot_nt(h_ref[...], ubuf[...].astype(BF16))
    row = lax.broadcasted_iota(I32, (PEER_TG, rows), 0)
    diag = (lax.broadcasted_iota(I32, (PEER_TG, rows), 1) // nk) == row
    res = jnp.where(diag, res, 0.0)
    a = res[:, 0:nk]
    for j in range(1, PEER_TG):
        a = a + res[:, j * nk:(j + 1) * nk]
    act = 0.5 * a * (1.0 + lax.erf(a * np.float32(2.0 ** -0.5)))
    coef = g_ref[...] * act
    coef_bd = jnp.where(diag, jnp.concatenate([coef] * PEER_TG, axis=1), 0.0).astype(BF16)
    out = _dot(coef_bd, vbuf[...].astype(BF16))
    y_ref[...] = _rms(x1_ref[...] + out, gf_ref[...])


def _peer(idx, g, h2, x1, gf, u, v):
    T = idx.shape[0]
    nk = idx.shape[1]
    tg = PEER_TG
    row = lambda w: pl.BlockSpec((tg, w), lambda i: (i, 0))
    return pl.pallas_call(
        _peer_kernel,
        grid=(T // tg,),
        in_specs=[pl.BlockSpec((tg, nk), lambda i: (i, 0), memory_space=pltpu.SMEM),
                  row(nk), row(D_MODEL), row(D_MODEL), pl.BlockSpec(gf.shape, lambda i: (0, 0)),
                  pl.BlockSpec(memory_space=pl.ANY), pl.BlockSpec(memory_space=pl.ANY)],
        out_specs=row(D_MODEL),
        out_shape=jax.ShapeDtypeStruct((T, D_MODEL), F32),
        scratch_shapes=[pltpu.VMEM((tg * nk, D_MODEL), F32), pltpu.VMEM((tg * nk, D_MODEL), F32),
                        pltpu.SemaphoreType.DMA((2,))],
        compiler_params=_params("arbitrary"),
        name="peer_ffn",
    )(idx, g, h2, x1, gf, u, v)


def _rope_tables(pos, rot, width):
    half = rot // 2
    inv_freq = ROPE_THETA ** (-(jnp.arange(half, dtype=F32) * 2.0) / rot)
    ang = pos.astype(F32)[:, None] * inv_freq
    cos, sin = jnp.cos(ang), jnp.sin(ang)
    T = pos.shape[0]
    one = jnp.ones((T, width - rot), F32)
    zero = jnp.zeros((T, width - rot), F32)
    zh = jnp.zeros((T, half), F32)
    c = jnp.concatenate([cos, cos, one], axis=1)
    sa = jnp.concatenate([zh, sin, zero], axis=1)
    sb = jnp.concatenate([-sin, zh, zero], axis=1)
    reps = LANES // width
    return jnp.stack([jnp.tile(t, (1, reps)) for t in (c, sa, sb)])


def kernel(x, positions, norm_mix_g, w_in, mla_q_norm_g, mla_kv_norm_g, mla_w_uq, mla_w_uk, mla_w_uv, w_branch_a, w_branch_b, w_out, norm_ffn_g, peer_w_q, peer_sub_keys, peer_u, peer_v, norm_final_g):
    B, S, D = x.shape
    T = B * S
    assert D == D_MODEL and w_in.shape[0] == 1 and S % 256 == 0 and T % 512 == 0
    xf = x.reshape(T, D)
    pos = positions.reshape(T)

    splits = [int(c) for c in np.cumsum(IN_SIZES)[:-1]]
    w_cq, w_ckv, w_kr, w_dq, w_dk, w_dv, w_iq, w_ik, w_iw, w_gates = jnp.split(w_in[0], splits, axis=1)
    zc = lambda n: jnp.zeros((D, n), F32)
    w_pack = jnp.concatenate([w_gates, w_dq, w_iq, w_cq, w_ckv, w_dk, w_dv, w_kr, zc(64), w_ik, w_iw, zc(48)],
                             axis=1).astype(BF16)
    wuq = jnp.pad(mla_w_uq[0], ((0, 0), (0, 0), (0, MLA_QK_PAD - MLA_NOPE - MLA_ROPE)))
    wuq = wuq.reshape(MLA_Q_RANK, MLA_HEADS * MLA_QK_PAD).astype(BF16)
    wuk = mla_w_uk[0].reshape(MLA_KV_RANK, MLA_HEADS * MLA_NOPE).astype(BF16)
    wuv = mla_w_uv[0].reshape(MLA_KV_RANK, MLA_HEADS * MLA_V).astype(BF16)
    wq_peer = peer_w_q[0].reshape(D, PEER_HEADS * PEER_KEY_DIM).astype(BF16)
    sk = peer_sub_keys[0].reshape(PEER_HEADS * 2, PEER_N_KEYS, PEER_KEY_DIM // 2)
    zk = jnp.zeros_like(sk)
    first = (jnp.arange(PEER_HEADS * 2) % 2 == 0)[:, None, None]
    keys_pad = jnp.where(first, jnp.concatenate([sk, zk], axis=2), jnp.concatenate([zk, sk], axis=2)).astype(BF16)

    tab_m = _rope_tables(pos, MLA_ROPE, LANES)
    tab_d = _rope_tables(pos, DSA_ROT, DSA_HEAD_DIM)
    tab_i = _rope_tables(pos, IDX_ROT, IDX_DIM)

    proj = _inproj(xf, norm_mix_g[0][None], w_pack)
    qm, km, vm, dq, dk, dv, iq, ka, kb = _prep(proj, tab_m, tab_d, tab_i, mla_q_norm_g[0][None],
                                               mla_kv_norm_g[0][None], wuq, wuk, wuv)
    o_a = _mla_attn(qm, km, vm, B, S)
    o_b = _dsa(iq, proj, dq, ka, kb, dk, dv, B, S)
    merged = _merge(o_a, o_b, proj, w_branch_a[0].astype(BF16), w_branch_b[0].astype(BF16))
    x1, h2, qp = _outproj(xf, merged, w_out[0].astype(BF16), norm_ffn_g[0][None], wq_peer)
    idx_t, g_t = _route(qp, keys_pad)
    nk = PEER_HEADS * PEER_TOPK
    idx = idx_t.transpose(0, 2, 1).reshape(T, nk)
    g = g_t.transpose(0, 2, 1).reshape(T, nk)
    y = _peer(idx, g, h2, x1, norm_final_g[None], peer_u[0], peer_v[0])
    return y.reshape(B, S, D)
```

```python
import functools

import numpy as np
import jax
import jax.numpy as jnp
from jax import lax
from jax.experimental import pallas as pl
from jax.experimental.pallas import tpu as pltpu

F32 = jnp.float32
BF16 = jnp.bfloat16
I32 = jnp.int32

D_MODEL = 2048
CHUNK = 64
Q_BLOCK = 128
ROPE_THETA = 500000.0
EPS = 1e-6
NEG_INF = -1e30

MLA_HEADS = 8
MLA_NOPE = 128
MLA_ROPE = 64
MLA_V = 128
MLA_Q_RANK = 512
MLA_KV_RANK = 256
MLA_QK_PAD = 256

DSA_HEADS = 8
DSA_KV_HEADS = 2
DSA_GROUP = DSA_HEADS // DSA_KV_HEADS
DSA_HEAD_DIM = 128
DSA_ROT = DSA_HEAD_DIM // 4
IDX_HEADS = 16
IDX_DIM = 64
IDX_ROT = IDX_DIM // 4
DSA_TOPK_MAX = 256

PEER_HEADS = 8
PEER_N_KEYS = 128
PEER_KEY_DIM = 128
PEER_TOPK = 16

IN_SIZES = (MLA_Q_RANK, MLA_KV_RANK, MLA_ROPE, DSA_HEADS * DSA_HEAD_DIM, DSA_KV_HEADS * DSA_HEAD_DIM,
            DSA_KV_HEADS * DSA_HEAD_DIM, IDX_HEADS * IDX_DIM, IDX_DIM, IDX_HEADS, 2 * D_MODEL)

LANES = 128

OFF_GATES = 0
OFF_DQ = 4096
OFF_IQ = 5120
OFF_CQ = 6144
OFF_CKV = 6656
OFF_DK = 6912
OFF_DV = 7168
OFF_KROPE = 7424
OFF_IKW = 7552
N_PACK = 7680

VMEM_LIMIT = 56 * 1024 * 1024


def _params(*sem):
    return pltpu.CompilerParams(dimension_semantics=sem, vmem_limit_bytes=VMEM_LIMIT)


def _rms(t, g):
    return t * lax.rsqrt(jnp.mean(t * t, axis=-1, keepdims=True) + EPS) * g


def _dot(a, b):
    return jnp.dot(a, b, preferred_element_type=F32)


def _dot_nt(a, b):
    return lax.dot_general(a, b, (((1,), (1,)), ((), ())), preferred_element_type=F32)


def _rope(t, c, sa, sb, half):
    return t * c + pltpu.roll(t, half, 1) * sa + pltpu.roll(t, LANES - half, 1) * sb


def _inproj_kernel(x_ref, g_ref, w_ref, o_ref, h_ref):
    @pl.when(pl.program_id(1) == 0)
    def _():
        h_ref[...] = _rms(x_ref[...], g_ref[...]).astype(BF16)

    o_ref[...] = _dot(h_ref[...], w_ref[...])


def _inproj(x, g, w, tm=512, tn=1920):
    T = x.shape[0]
    return pl.pallas_call(
        _inproj_kernel,
        grid=(T // tm, N_PACK // tn),
        in_specs=[pl.BlockSpec((tm, D_MODEL), lambda i, j: (i, 0)),
                  pl.BlockSpec((1, D_MODEL), lambda i, j: (0, 0)),
                  pl.BlockSpec((D_MODEL, tn), lambda i, j: (0, j))],
        out_specs=pl.BlockSpec((tm, tn), lambda i, j: (i, j)),
        out_shape=jax.ShapeDtypeStruct((T, N_PACK), F32),
        scratch_shapes=[pltpu.VMEM((tm, D_MODEL), BF16)],
        compiler_params=_params("parallel", "arbitrary"),
        name="inproj",
    )(x, g, w)


def _prep_kernel(cq_ref, ckv_ref, kr_ref, dq_ref, dk_ref, dv_ref, iq_ref, ikw_ref,
                 tm_ref, td_ref, ti_ref, gq_ref, gkv_ref, wuq_ref, wuk_ref, wuv_ref,
                 qm_ref, km_ref, vm_ref, dqo_ref, dko_ref, dvo_ref, iqo_ref, ka_ref, kb_ref):
    cm, sam, sbm = tm_ref[0], tm_ref[1], tm_ref[2]
    cd, sad, sbd = td_ref[0], td_ref[1], td_ref[2]
    ci, sai, sbi = ti_ref[0], ti_ref[1], ti_ref[2]

    qn = _rms(cq_ref[...], gq_ref[...]).astype(BF16)
    q = _dot(qn, wuq_ref[...])
    for h in range(MLA_HEADS):
        lo = h * MLA_QK_PAD
        qm_ref[:, lo:lo + LANES] = q[:, lo:lo + LANES].astype(BF16)
        qm_ref[:, lo + LANES:lo + 2 * LANES] = _rope(q[:, lo + LANES:lo + 2 * LANES], cm, sam, sbm,
                                                       MLA_ROPE // 2).astype(BF16)
    ckv = _rms(ckv_ref[...], gkv_ref[...]).astype(BF16)
    kn = _dot(ckv, wuk_ref[...])
    kpe = _rope(kr_ref[...], cm, sam, sbm, MLA_ROPE // 2).astype(BF16)
    for h in range(MLA_HEADS):
        lo = h * MLA_QK_PAD
        km_ref[:, lo:lo + LANES] = kn[:, h * MLA_NOPE:(h + 1) * MLA_NOPE].astype(BF16)
        km_ref[:, lo + LANES:lo + 2 * LANES] = kpe
    vm_ref[...] = _dot(ckv, wuv_ref[...]).astype(BF16)

    for h in range(DSA_HEADS):
        sl = slice(h * LANES, (h + 1) * LANES)
        dqo_ref[:, sl] = _rope(dq_ref[:, sl], cd, sad, sbd, DSA_ROT // 2).astype(BF16)
    for h in range(DSA_KV_HEADS):
        sl = slice(h * LANES, (h + 1) * LANES)
        dko_ref[:, sl] = _rope(dk_ref[:, sl], cd, sad, sbd, DSA_ROT // 2).astype(BF16)
    dvo_ref[...] = dv_ref[...].astype(BF16)

    for c in range(IDX_HEADS * IDX_DIM // LANES):
        sl = slice(c * LANES, (c + 1) * LANES)
        iqo_ref[:, sl] = _rope(iq_ref[:, sl], ci, sai, sbi, IDX_ROT // 2).astype(BF16)
    lane = lax.broadcasted_iota(I32, ikw_ref.shape, 1)
    ka = jnp.where(lane < IDX_DIM, _rope(ikw_ref[...], ci, sai, sbi, IDX_ROT // 2), 0.0)
    ka_ref[...] = ka.astype(BF16)
    kb_ref[...] = pltpu.roll(ka, IDX_DIM, 1).astype(BF16)


def _prep(proj, tab_m, tab_d, tab_i, gq, gkv, wuq, wuk, wuv, tm=256):
    T = proj.shape[0]

    def col(width, off):
        return pl.BlockSpec((tm, width), lambda i: (i, off // width))

    def full(a):
        return pl.BlockSpec(a.shape, lambda i: (0,) * a.ndim)

    tab = pl.BlockSpec((3, tm, LANES), lambda i: (0, i, 0))
    widths = (MLA_HEADS * MLA_QK_PAD, MLA_HEADS * MLA_QK_PAD, MLA_HEADS * MLA_V, DSA_HEADS * DSA_HEAD_DIM,
              DSA_KV_HEADS * DSA_HEAD_DIM, DSA_KV_HEADS * DSA_HEAD_DIM, IDX_HEADS * IDX_DIM, LANES, LANES)
    return pl.pallas_call(
        _prep_kernel,
        grid=(T // tm,),
        in_specs=[col(MLA_Q_RANK, OFF_CQ), col(MLA_KV_RANK, OFF_CKV), col(LANES, OFF_KROPE),
                  col(1024, OFF_DQ), col(256, OFF_DK), col(256, OFF_DV), col(1024, OFF_IQ), col(LANES, OFF_IKW),
                  tab, tab, tab, full(gq), full(gkv), full(wuq), full(wuk), full(wuv)],
        out_specs=[pl.BlockSpec((tm, w), lambda i: (i, 0)) for w in widths],
        out_shape=[jax.ShapeDtypeStruct((T, w), BF16) for w in widths],
        compiler_params=_params("parallel"),
        name="prep",
    )(proj, proj, proj, proj, proj, proj, proj, proj, tab_m, tab_d, tab_i, gq, gkv, wuq, wuk, wuv)


def _mla_attn_kernel(q_ref, k_ref, v_ref, o_ref, *, tq, scale):
    S = k_ref.shape[0]
    q0 = pl.program_id(1) * tq
    q_chunk = (q0 + lax.broadcasted_iota(I32, (tq, S), 0)) // CHUNK
    k_chunk = lax.broadcasted_iota(I32, (tq, S), 1) // CHUNK
    allowed = k_chunk <= q_chunk
    for h in range(MLA_HEADS):
        qk = slice(h * MLA_QK_PAD, (h + 1) * MLA_QK_PAD)
        s = _dot_nt(q_ref[:, qk], k_ref[:, qk]) * scale
        s = jnp.where(allowed, s, NEG_INF)
        p = jnp.exp(s - jnp.max(s, axis=-1, keepdims=True))
        l = jnp.sum(p, axis=-1, keepdims=True)
        vs = slice(h * MLA_V, (h + 1) * MLA_V)
        o_ref[:, vs] = (_dot(p.astype(BF16), v_ref[:, vs]) / l).astype(BF16)


def _mla_attn(q, k, v, B, S, tq=256):
    nq = S // tq
    return pl.pallas_call(
        functools.partial(_mla_attn_kernel, tq=tq, scale=(MLA_NOPE + MLA_ROPE) ** -0.5),
        grid=(B, nq),
        in_specs=[pl.BlockSpec((tq, q.shape[1]), lambda b, i: (b * nq + i, 0)),
                  pl.BlockSpec((S, k.shape[1]), lambda b, i: (b, 0)),
                  pl.BlockSpec((S, v.shape[1]), lambda b, i: (b, 0))],
        out_specs=pl.BlockSpec((tq, v.shape[1]), lambda b, i: (b * nq + i, 0)),
        out_shape=jax.ShapeDtypeStruct((B * S, v.shape[1]), BF16),
        compiler_params=_params("parallel", "arbitrary"),
        name="mla_attn",
    )(q, k, v)


def _order_key(score):
    bits = pltpu.bitcast(score + 0.0, I32)
    return bits ^ ((bits >> 31) & jnp.int32(0x7FFFFFFF))


def _dsa_kernel(iq_ref, ikw_ref, dq_ref, ka_ref, kb_ref, dk_ref, dv_ref, o_ref, *, top_k, scale):
    S = ka_ref.shape[0]
    tq = Q_BLOCK
    q0 = pl.program_id(1) * tq
    q_chunk = (q0 + lax.broadcasted_iota(I32, (tq, S), 0)) // CHUNK
    k_chunk = lax.broadcasted_iota(I32, (tq, S), 1) // CHUNK
    allowed = k_chunk <= q_chunk

    w = ikw_ref[...] * (IDX_HEADS ** -0.5)
    score = jnp.zeros((tq, S), F32)
    for c in range(IDX_HEADS // 2):
        qpair = iq_ref[:, c * LANES:(c + 1) * LANES]
        for half, k_ref in enumerate((ka_ref, kb_ref)):
            h = 2 * c + half
            dots = _dot_nt(qpair, k_ref[...]) * (IDX_DIM ** -0.5)
            score = score + w[:, IDX_DIM + h:IDX_DIM + h + 1] * jnp.maximum(dots, 0.0)
    score = jnp.where(allowed, score, NEG_INF)

    keys = _order_key(score)
    kf = jnp.float32(top_k)

    def count_ge(cand):
        return jnp.sum((keys >= cand).astype(F32), axis=-1, keepdims=True)

    int_min = jnp.full((tq, 1), jnp.iinfo(jnp.int32).min, I32)
    thr = jnp.where(count_ge(jnp.zeros((tq, 1), I32)) >= kf, 0, int_min)

    def step(i, thr):
        cand = thr + jnp.left_shift(jnp.int32(1), 30 - i)
        return jnp.where(count_ge(cand) >= kf, cand, thr)

    thr = lax.fori_loop(0, 31, step, thr)

    gt = keys > thr
    eq = keys == thr
    need = kf - jnp.sum(gt.astype(F32), axis=-1, keepdims=True)
    tri = (lax.broadcasted_iota(I32, (LANES, LANES), 0) <= lax.broadcasted_iota(I32, (LANES, LANES), 1)).astype(BF16)
    run = jnp.zeros((tq, 1), F32)
    sel = []
    for c in range(S // LANES):
        sl = slice(c * LANES, (c + 1) * LANES)
        prefix = _dot(eq[:, sl].astype(BF16), tri) + run
        sel.append(gt[:, sl] | (eq[:, sl] & (prefix <= need)))
        run = prefix[:, LANES - 1:LANES]
    mask = jnp.concatenate(sel, axis=1) & allowed

    mask_g = jnp.concatenate([mask] * DSA_GROUP, axis=0)
    for g in range(DSA_KV_HEADS):
        kv = slice(g * DSA_HEAD_DIM, (g + 1) * DSA_HEAD_DIM)
        qg = jnp.concatenate([dq_ref[:, (g * DSA_GROUP + n) * DSA_HEAD_DIM:(g * DSA_GROUP + n + 1) * DSA_HEAD_DIM]
                              for n in range(DSA_GROUP)], axis=0)
        s = _dot_nt(qg, dk_ref[:, kv]) * scale
        s = jnp.where(mask_g, s, NEG_INF)
        p = jnp.exp(s - jnp.max(s, axis=-1, keepdims=True))
        l = jnp.sum(p, axis=-1, keepdims=True)
        o = _dot(p.astype(BF16), dv_ref[:, kv]) / l
        for n in range(DSA_GROUP):
            hs = (g * DSA_GROUP + n) * DSA_HEAD_DIM
            o_ref[:, hs:hs + DSA_HEAD_DIM] = o[n * tq:(n + 1) * tq].astype(BF16)


def _dsa(iq, proj, dq, ka, kb, dk, dv, B, S):
    tq = Q_BLOCK
    nq = S // tq

    def qblk(width, off=0):
        return pl.BlockSpec((tq, width), lambda b, i: (b * nq + i, off // width))

    def kblk(width):
        return pl.BlockSpec((S, width), lambda b, i: (b, 0))

    return pl.pallas_call(
        functools.partial(_dsa_kernel, top_k=min(DSA_TOPK_MAX, S // 4), scale=DSA_HEAD_DIM ** -0.5),
        grid=(B, nq),
        in_specs=[qblk(1024), qblk(LANES, OFF_IKW), qblk(1024), kblk(LANES), kblk(LANES), kblk(256), kblk(256)],
        out_specs=qblk(1024),
        out_shape=jax.ShapeDtypeStruct((B * S, DSA_HEADS * DSA_HEAD_DIM), BF16),
        compiler_params=_params("parallel", "arbitrary"),
        name="dsa",
    )(iq, proj, dq, ka, kb, dk, dv)


def _merge_kernel(oa_ref, ob_ref, ga_ref, gb_ref, wa_ref, wb_ref, o_ref):
    a = _dot(oa_ref[...], wa_ref[...])
    b = _dot(ob_ref[...], wb_ref[...])
    o_ref[...] = (jax.nn.sigmoid(ga_ref[...]) * a + jax.nn.sigmoid(gb_ref[...]) * b).astype(BF16)


def _merge(oa, ob, proj, wa, wb, tm=256):
    T = oa.shape[0]
    row = lambda w, c=0: pl.BlockSpec((tm, w), lambda i: (i, c))
    full = lambda a: pl.BlockSpec(a.shape, lambda i: (0, 0))
    return pl.pallas_call(
        _merge_kernel,
        grid=(T // tm,),
        in_specs=[row(1024), row(1024), row(D_MODEL, 0), row(D_MODEL, 1), full(wa), full(wb)],
        out_specs=row(D_MODEL),
        out_shape=jax.ShapeDtypeStruct((T, D_MODEL), BF16),
        compiler_params=_params("parallel"),
        name="merge",
    )(oa, ob, proj, proj, wa, wb)


def _outproj_kernel(x_ref, m_ref, wo_ref, g_ref, wq_ref, x1_ref, h2_ref, qp_ref):
    x1 = x_ref[...] + _dot(m_ref[...], wo_ref[...])
    x1_ref[...] = x1
    h2 = _rms(x1, g_ref[...]).astype(BF16)
    h2_ref[...] = h2
    qp_ref[...] = _dot(h2, wq_ref[...]).astype(BF16)


def _outproj(x, merged, wo, g, wq, tm=256):
    T = x.shape[0]
    row = lambda w: pl.BlockSpec((tm, w), lambda i: (i, 0))
    full = lambda a: pl.BlockSpec(a.shape, lambda i: (0, 0))
    nq = wq.shape[1]
    return pl.pallas_call(
        _outproj_kernel,
        grid=(T // tm,),
        in_specs=[row(D_MODEL), row(D_MODEL), full(wo), full(g), full(wq)],
        out_specs=[row(D_MODEL), row(D_MODEL), row(nq)],
        out_shape=[jax.ShapeDtypeStruct((T, D_MODEL), F32), jax.ShapeDtypeStruct((T, D_MODEL), BF16),
                   jax.ShapeDtypeStruct((T, nq), BF16)],
        compiler_params=_params("parallel"),
        name="outproj",
    )(x, merged, wo, g, wq)


def _top16(cur, payload=None):
    n = cur.shape[0]
    iota = lax.broadcasted_iota(I32, cur.shape, 0)
    vals, idxs = [], []
    for _ in range(PEER_TOPK):
        m = jnp.max(cur, axis=0, keepdims=True)
        idx = jnp.min(jnp.where(cur == m, iota, n), axis=0, keepdims=True)
        hit = iota == idx
        vals.append(m)
        if payload is None:
            idxs.append(idx)
        else:
            idxs.append(jnp.max(jnp.where(hit, payload, -1), axis=0, keepdims=True))
        cur = jnp.where(hit, -jnp.inf, cur)
    return jnp.concatenate(vals, axis=0), jnp.concatenate(idxs, axis=0)


def _route_kernel(qp_ref, keys_ref, idx_ref, g_ref):
    for h in range(PEER_HEADS):
        qpair = qp_ref[:, h * PEER_KEY_DIM:(h + 1) * PEER_KEY_DIM]
        s1, i1 = _top16(_dot_nt(keys_ref[2 * h], qpair))
        s2, i2 = _top16(_dot_nt(keys_ref[2 * h + 1], qpair))
        cand_s = jnp.concatenate([s1[a:a + 1] + s2 for a in range(PEER_TOPK)], axis=0)
        cand_i = jnp.concatenate([i1[a:a + 1] * PEER_N_KEYS + i2 for a in range(PEER_TOPK)], axis=0)
        top_s, expert = _top16(cand_s, cand_i)
        e = jnp.exp(top_s - top_s[0:1])
        g = e / jnp.sum(e, axis=0, keepdims=True)
        idx_ref[0, h * PEER_TOPK:(h + 1) * PEER_TOPK, :] = expert
        g_ref[0, h * PEER_TOPK:(h + 1) * PEER_TOPK, :] = g


def _route(qp, keys_pad, tb=128):
    T = qp.shape[0]
    nb = T // tb
    nk = PEER_HEADS * PEER_TOPK
    out = pl.BlockSpec((1, nk, tb), lambda i: (i, 0, 0))
    return pl.pallas_call(
        _route_kernel,
        grid=(nb,),
        in_specs=[pl.BlockSpec((tb, qp.shape[1]), lambda i: (i, 0)),
                  pl.BlockSpec(keys_pad.shape, lambda i: (0, 0, 0))],
        out_specs=[out, out],
        out_shape=[jax.ShapeDtypeStruct((nb, nk, tb), I32), jax.ShapeDtypeStruct((nb, nk, tb), F32)],
        compiler_params=_params("parallel"),
        name="peer_route",
    )(qp, keys_pad)


PEER_TG = 8


def _peer_issue(idx_ref, t0, uv_hbm, buf, sem):
    nk = PEER_HEADS * PEER_TOPK
    for t in range(PEER_TG):
        for k in range(nk):
            pltpu.async_copy(uv_hbm.at[pl.ds(idx_ref[t0 + t, k], 1)], buf.at[pl.ds(t * nk + k, 1)], sem)


def _peer_wait(uv_hbm, buf, sem):
    pltpu.make_async_copy(uv_hbm.at[pl.ds(0, buf.shape[0])], buf, sem).wait()


def _peer_compute(buf, g, h, x1, gf):
    nk = PEER_HEADS * PEER_TOPK
    rows = PEER_TG * nk
    res = _dot_nt(h, buf[:, 0:D_MODEL].astype(BF16))
    row = lax.broadcasted_iota(I32, (PEER_TG, rows), 0)
    diag = (lax.broadcasted_iota(I32, (PEER_TG, rows), 1) // nk) == row
    res = jnp.where(diag, res, 0.0)
    a = res[:, 0:nk]
    for j in range(1, PEER_TG):
        a = a + res[:, j * nk:(j + 1) * nk]
    act = 0.5 * a * (1.0 + lax.erf(a * np.float32(2.0 ** -0.5)))
    coef = g * act
    coef_bd = jnp.where(diag, jnp.concatenate([coef] * PEER_TG, axis=1), 0.0).astype(BF16)
    out = _dot(coef_bd, buf[:, D_MODEL:2 * D_MODEL].astype(BF16))
    return _rms(x1 + out, gf)


def _peer_kernel(idx_ref, nxt_ref, g_ref, h_ref, x1_ref, gf_ref, uv_hbm, y_ref, buf_a, buf_b, sem):
    i = pl.program_id(0)
    lo, hi = slice(0, PEER_TG), slice(PEER_TG, 2 * PEER_TG)

    @pl.when(i == 0)
    def _():
        _peer_issue(idx_ref, 0, uv_hbm, buf_a, sem.at[0])

    _peer_issue(idx_ref, PEER_TG, uv_hbm, buf_b, sem.at[1])
    _peer_wait(uv_hbm, buf_a, sem.at[0])
    y_ref[lo, :] = _peer_compute(buf_a, g_ref[lo, :], h_ref[lo, :], x1_ref[lo, :], gf_ref[...])
    _peer_issue(nxt_ref, 0, uv_hbm, buf_a, sem.at[0])
    _peer_wait(uv_hbm, buf_b, sem.at[1])
    y_ref[hi, :] = _peer_compute(buf_b, g_ref[hi, :], h_ref[hi, :], x1_ref[hi, :], gf_ref[...])

    @pl.when(i == pl.num_programs(0) - 1)
    def _():
        _peer_wait(uv_hbm, buf_a, sem.at[0])


def _peer(idx, g, h2, x1, gf, uv):
    T = idx.shape[0]
    nk = idx.shape[1]
    ts = 2 * PEER_TG
    n = T // ts
    row = lambda w: pl.BlockSpec((ts, w), lambda i: (i, 0))
    return pl.pallas_call(
        _peer_kernel,
        grid=(n,),
        in_specs=[pl.BlockSpec((ts, nk), lambda i: (i, 0), memory_space=pltpu.SMEM),
                  pl.BlockSpec((ts, nk), lambda i: (jnp.minimum(i + 1, n - 1), 0), memory_space=pltpu.SMEM),
                  row(nk), row(D_MODEL), row(D_MODEL), pl.BlockSpec(gf.shape, lambda i: (0, 0)),
                  pl.BlockSpec(memory_space=pl.ANY)],
        out_specs=row(D_MODEL),
        out_shape=jax.ShapeDtypeStruct((T, D_MODEL), F32),
        scratch_shapes=[pltpu.VMEM((PEER_TG * nk, 2 * D_MODEL), F32), pltpu.VMEM((PEER_TG * nk, 2 * D_MODEL), F32),
                        pltpu.SemaphoreType.DMA((2,))],
        compiler_params=_params("arbitrary"),
        name="peer_ffn",
    )(idx, idx, g, h2, x1, gf, uv)


def _rope_tables(pos, rot, width):
    half = rot // 2
    inv_freq = ROPE_THETA ** (-(jnp.arange(half, dtype=F32) * 2.0) / rot)
    ang = pos.astype(F32)[:, None] * inv_freq
    cos, sin = jnp.cos(ang), jnp.sin(ang)
    T = pos.shape[0]
    one = jnp.ones((T, width - rot), F32)
    zero = jnp.zeros((T, width - rot), F32)
    zh = jnp.zeros((T, half), F32)
    c = jnp.concatenate([cos, cos, one], axis=1)
    sa = jnp.concatenate([zh, sin, zero], axis=1)
    sb = jnp.concatenate([-sin, zh, zero], axis=1)
    reps = LANES // width
    return jnp.stack([jnp.tile(t, (1, reps)) for t in (c, sa, sb)])


def kernel(x, positions, norm_mix_g, w_in, mla_q_norm_g, mla_kv_norm_g, mla_w_uq, mla_w_uk, mla_w_uv, w_branch_a, w_branch_b, w_out, norm_ffn_g, peer_w_q, peer_sub_keys, peer_u, peer_v, norm_final_g):
    B, S, D = x.shape
    T = B * S
    assert D == D_MODEL and w_in.shape[0] == 1 and S % 256 == 0 and T % 512 == 0
    xf = x.reshape(T, D)
    pos = positions.reshape(T)

    splits = [int(c) for c in np.cumsum(IN_SIZES)[:-1]]
    w_cq, w_ckv, w_kr, w_dq, w_dk, w_dv, w_iq, w_ik, w_iw, w_gates = jnp.split(w_in[0], splits, axis=1)
    zc = lambda n: jnp.zeros((D, n), F32)
    w_pack = jnp.concatenate([w_gates, w_dq, w_iq, w_cq, w_ckv, w_dk, w_dv, w_kr, zc(64), w_ik, w_iw, zc(48)],
                             axis=1).astype(BF16)
    wuq = jnp.pad(mla_w_uq[0], ((0, 0), (0, 0), (0, MLA_QK_PAD - MLA_NOPE - MLA_ROPE)))
    wuq = wuq.reshape(MLA_Q_RANK, MLA_HEADS * MLA_QK_PAD).astype(BF16)
    wuk = mla_w_uk[0].reshape(MLA_KV_RANK, MLA_HEADS * MLA_NOPE).astype(BF16)
    wuv = mla_w_uv[0].reshape(MLA_KV_RANK, MLA_HEADS * MLA_V).astype(BF16)
    wq_peer = peer_w_q[0].reshape(D, PEER_HEADS * PEER_KEY_DIM).astype(BF16)
    sk = peer_sub_keys[0].reshape(PEER_HEADS * 2, PEER_N_KEYS, PEER_KEY_DIM // 2)
    zk = jnp.zeros_like(sk)
    first = (jnp.arange(PEER_HEADS * 2) % 2 == 0)[:, None, None]
    keys_pad = jnp.where(first, jnp.concatenate([sk, zk], axis=2), jnp.concatenate([zk, sk], axis=2)).astype(BF16)

    tab_m = _rope_tables(pos, MLA_ROPE, LANES)
    tab_d = _rope_tables(pos, DSA_ROT, DSA_HEAD_DIM)
    tab_i = _rope_tables(pos, IDX_ROT, IDX_DIM)

    proj = _inproj(xf, norm_mix_g[0][None], w_pack)
    qm, km, vm, dq, dk, dv, iq, ka, kb = _prep(proj, tab_m, tab_d, tab_i, mla_q_norm_g[0][None],
                                               mla_kv_norm_g[0][None], wuq, wuk, wuv)
    o_a = _mla_attn(qm, km, vm, B, S)
    o_b = _dsa(iq, proj, dq, ka, kb, dk, dv, B, S)
    merged = _merge(o_a, o_b, proj, w_branch_a[0].astype(BF16), w_branch_b[0].astype(BF16))
    x1, h2, qp = _outproj(xf, merged, w_out[0].astype(BF16), norm_ffn_g[0][None], wq_peer)
    idx_t, g_t = _route(qp, keys_pad)
    nk = PEER_HEADS * PEER_TOPK
    idx = idx_t.transpose(0, 2, 1).reshape(T, nk)
    g = g_t.transpose(0, 2, 1).reshape(T, nk)
    uv = jnp.concatenate([peer_u[0], peer_v[0]], axis=1)
    y = _peer(idx, g, h2, x1, norm_final_g[None], uv)
    return y.reshape(B, S, D)
```

```python
import functools

import numpy as np
import jax
import jax.numpy as jnp
from jax import lax
from jax.experimental import pallas as pl
from jax.experimental.pallas import tpu as pltpu

F32 = jnp.float32
BF16 = jnp.bfloat16
I32 = jnp.int32

D_MODEL = 2048
CHUNK = 64
Q_BLOCK = 128
ROPE_THETA = 500000.0
EPS = 1e-6
NEG_INF = -1e30

MLA_HEADS = 8
MLA_NOPE = 128
MLA_ROPE = 64
MLA_V = 128
MLA_Q_RANK = 512
MLA_KV_RANK = 256
MLA_QK_PAD = 256

DSA_HEADS = 8
DSA_KV_HEADS = 2
DSA_GROUP = DSA_HEADS // DSA_KV_HEADS
DSA_HEAD_DIM = 128
DSA_ROT = DSA_HEAD_DIM // 4
IDX_HEADS = 16
IDX_DIM = 64
IDX_ROT = IDX_DIM // 4
DSA_TOPK_MAX = 256

PEER_HEADS = 8
PEER_N_KEYS = 128
PEER_KEY_DIM = 128
PEER_TOPK = 16

IN_SIZES = (MLA_Q_RANK, MLA_KV_RANK, MLA_ROPE, DSA_HEADS * DSA_HEAD_DIM, DSA_KV_HEADS * DSA_HEAD_DIM,
            DSA_KV_HEADS * DSA_HEAD_DIM, IDX_HEADS * IDX_DIM, IDX_DIM, IDX_HEADS, 2 * D_MODEL)

LANES = 128

OFF_GATES = 0
OFF_DQ = 4096
OFF_IQ = 5120
OFF_CQ = 6144
OFF_CKV = 6656
OFF_DK = 6912
OFF_DV = 7168
OFF_KROPE = 7424
OFF_IKW = 7552
N_PACK = 7680

VMEM_LIMIT = 56 * 1024 * 1024


def _params(*sem):
    return pltpu.CompilerParams(dimension_semantics=sem, vmem_limit_bytes=VMEM_LIMIT)


def _rms(t, g):
    return t * lax.rsqrt(jnp.mean(t * t, axis=-1, keepdims=True) + EPS) * g


def _dot(a, b):
    return jnp.dot(a, b, preferred_element_type=F32)


def _dot_nt(a, b):
    return lax.dot_general(a, b, (((1,), (1,)), ((), ())), preferred_element_type=F32)


def _rope(t, c, sa, sb, half):
    return t * c + pltpu.roll(t, half, 1) * sa + pltpu.roll(t, LANES - half, 1) * sb


def _inproj_kernel(x_ref, g_ref, w_ref, o_ref, h_ref):
    @pl.when(pl.program_id(1) == 0)
    def _():
        h_ref[...] = _rms(x_ref[...], g_ref[...]).astype(BF16)

    o_ref[...] = _dot(h_ref[...], w_ref[...])


def _inproj(x, g, w, tm=512, tn=1920):
    T = x.shape[0]
    return pl.pallas_call(
        _inproj_kernel,
        grid=(T // tm, N_PACK // tn),
        in_specs=[pl.BlockSpec((tm, D_MODEL), lambda i, j: (i, 0)),
                  pl.BlockSpec((1, D_MODEL), lambda i, j: (0, 0)),
                  pl.BlockSpec((D_MODEL, tn), lambda i, j: (0, j))],
        out_specs=pl.BlockSpec((tm, tn), lambda i, j: (i, j)),
        out_shape=jax.ShapeDtypeStruct((T, N_PACK), F32),
        scratch_shapes=[pltpu.VMEM((tm, D_MODEL), BF16)],
        compiler_params=_params("parallel", "arbitrary"),
        name="inproj",
    )(x, g, w)


def _prep_kernel(cq_ref, ckv_ref, kr_ref, dq_ref, dk_ref, dv_ref, iq_ref, ikw_ref,
                 tm_ref, td_ref, ti_ref, gq_ref, gkv_ref, wuq_ref, wuk_ref, wuv_ref,
                 qm_ref, km_ref, vm_ref, dqo_ref, dko_ref, dvo_ref, iqo_ref, ka_ref, kb_ref):
    cm, sam, sbm = tm_ref[0], tm_ref[1], tm_ref[2]
    cd, sad, sbd = td_ref[0], td_ref[1], td_ref[2]
    ci, sai, sbi = ti_ref[0], ti_ref[1], ti_ref[2]

    qn = _rms(cq_ref[...], gq_ref[...]).astype(BF16)
    q = _dot(qn, wuq_ref[...])
    for h in range(MLA_HEADS):
        lo = h * MLA_QK_PAD
        qm_ref[:, lo:lo + LANES] = q[:, lo:lo + LANES].astype(BF16)
        qm_ref[:, lo + LANES:lo + 2 * LANES] = _rope(q[:, lo + LANES:lo + 2 * LANES], cm, sam, sbm,
                                                       MLA_ROPE // 2).astype(BF16)
    ckv = _rms(ckv_ref[...], gkv_ref[...]).astype(BF16)
    kn = _dot(ckv, wuk_ref[...])
    kpe = _rope(kr_ref[...], cm, sam, sbm, MLA_ROPE // 2).astype(BF16)
    for h in range(MLA_HEADS):
        lo = h * MLA_QK_PAD
        km_ref[:, lo:lo + LANES] = kn[:, h * MLA_NOPE:(h + 1) * MLA_NOPE].astype(BF16)
        km_ref[:, lo + LANES:lo + 2 * LANES] = kpe
    vm_ref[...] = _dot(ckv, wuv_ref[...]).astype(BF16)

    for h in range(DSA_HEADS):
        sl = slice(h * LANES, (h + 1) * LANES)
        dqo_ref[:, sl] = _rope(dq_ref[:, sl], cd, sad, sbd, DSA_ROT // 2).astype(BF16)
    for h in range(DSA_KV_HEADS):
        sl = slice(h * LANES, (h + 1) * LANES)
        dko_ref[:, sl] = _rope(dk_ref[:, sl], cd, sad, sbd, DSA_ROT // 2).astype(BF16)
    dvo_ref[...] = dv_ref[...].astype(BF16)

    for c in range(IDX_HEADS * IDX_DIM // LANES):
        sl = slice(c * LANES, (c + 1) * LANES)
        iqo_ref[:, sl] = _rope(iq_ref[:, sl], ci, sai, sbi, IDX_ROT // 2).astype(BF16)
    lane = lax.broadcasted_iota(I32, ikw_ref.shape, 1)
    ka = jnp.where(lane < IDX_DIM, _rope(ikw_ref[...], ci, sai, sbi, IDX_ROT // 2), 0.0)
    ka_ref[...] = ka.astype(BF16)
    kb_ref[...] = pltpu.roll(ka, IDX_DIM, 1).astype(BF16)


def _prep(proj, tab_m, tab_d, tab_i, gq, gkv, wuq, wuk, wuv, tm=256):
    T = proj.shape[0]

    def col(width, off):
        return pl.BlockSpec((tm, width), lambda i: (i, off // width))

    def full(a):
        return pl.BlockSpec(a.shape, lambda i: (0,) * a.ndim)

    tab = pl.BlockSpec((3, tm, LANES), lambda i: (0, i, 0))
    widths = (MLA_HEADS * MLA_QK_PAD, MLA_HEADS * MLA_QK_PAD, MLA_HEADS * MLA_V, DSA_HEADS * DSA_HEAD_DIM,
              DSA_KV_HEADS * DSA_HEAD_DIM, DSA_KV_HEADS * DSA_HEAD_DIM, IDX_HEADS * IDX_DIM, LANES, LANES)
    return pl.pallas_call(
        _prep_kernel,
        grid=(T // tm,),
        in_specs=[col(MLA_Q_RANK, OFF_CQ), col(MLA_KV_RANK, OFF_CKV), col(LANES, OFF_KROPE),
                  col(1024, OFF_DQ), col(256, OFF_DK), col(256, OFF_DV), col(1024, OFF_IQ), col(LANES, OFF_IKW),
                  tab, tab, tab, full(gq), full(gkv), full(wuq), full(wuk), full(wuv)],
        out_specs=[pl.BlockSpec((tm, w), lambda i: (i, 0)) for w in widths],
        out_shape=[jax.ShapeDtypeStruct((T, w), BF16) for w in widths],
        compiler_params=_params("parallel"),
        name="prep",
    )(proj, proj, proj, proj, proj, proj, proj, proj, tab_m, tab_d, tab_i, gq, gkv, wuq, wuk, wuv)


def _mla_attn_tile(q_ref, k_ref, v_ref, o_ref, c, tq, scale):
    n = (c + 1) * tq
    q_chunk = (c * tq + lax.broadcasted_iota(I32, (tq, n), 0)) // CHUNK
    k_chunk = lax.broadcasted_iota(I32, (tq, n), 1) // CHUNK
    allowed = k_chunk <= q_chunk
    for h in range(MLA_HEADS):
        qk = slice(h * MLA_QK_PAD, (h + 1) * MLA_QK_PAD)
        s = _dot_nt(q_ref[:, qk], k_ref[0:n, qk]) * scale
        s = jnp.where(allowed, s, NEG_INF)
        p = jnp.exp(s - jnp.max(s, axis=-1, keepdims=True))
        l = jnp.sum(p, axis=-1, keepdims=True)
        vs = slice(h * MLA_V, (h + 1) * MLA_V)
        o_ref[:, vs] = (_dot(p.astype(BF16), v_ref[0:n, vs]) / l).astype(BF16)


def _mla_attn_kernel(q_ref, k_ref, v_ref, o_ref, *, tq, scale):
    for c in range(k_ref.shape[0] // tq):
        pl.when(pl.program_id(1) == c)(functools.partial(_mla_attn_tile, q_ref, k_ref, v_ref, o_ref, c, tq, scale))


def _mla_attn(q, k, v, B, S, tq=256):
    nq = S // tq
    return pl.pallas_call(
        functools.partial(_mla_attn_kernel, tq=tq, scale=(MLA_NOPE + MLA_ROPE) ** -0.5),
        grid=(B, nq),
        in_specs=[pl.BlockSpec((tq, q.shape[1]), lambda b, i: (b * nq + i, 0)),
                  pl.BlockSpec((S, k.shape[1]), lambda b, i: (b, 0)),
                  pl.BlockSpec((S, v.shape[1]), lambda b, i: (b, 0))],
        out_specs=pl.BlockSpec((tq, v.shape[1]), lambda b, i: (b * nq + i, 0)),
        out_shape=jax.ShapeDtypeStruct((B * S, v.shape[1]), BF16),
        compiler_params=_params("parallel", "arbitrary"),
        name="mla_attn",
    )(q, k, v)


def _order_key(score):
    bits = pltpu.bitcast(score + 0.0, I32)
    return bits ^ ((bits >> 31) & jnp.int32(0x7FFFFFFF))


def _dsa_body(iq_ref, ikw_ref, dq_ref, ka_ref, kb_ref, dk_ref, dv_ref, o_ref, S, top_k, scale):
    tq = Q_BLOCK
    q0 = pl.program_id(1) * tq
    q_chunk = (q0 + lax.broadcasted_iota(I32, (tq, S), 0)) // CHUNK
    k_chunk = lax.broadcasted_iota(I32, (tq, S), 1) // CHUNK
    allowed = k_chunk <= q_chunk

    w = ikw_ref[...] * (IDX_HEADS ** -0.5)
    score = jnp.zeros((tq, S), F32)
    for c in range(IDX_HEADS // 2):
        qpair = iq_ref[:, c * LANES:(c + 1) * LANES]
        for half, k_ref in enumerate((ka_ref, kb_ref)):
            h = 2 * c + half
            dots = _dot_nt(qpair, k_ref[0:S, :]) * (IDX_DIM ** -0.5)
            score = score + w[:, IDX_DIM + h:IDX_DIM + h + 1] * jnp.maximum(dots, 0.0)
    score = jnp.where(allowed, score, NEG_INF)

    keys = _order_key(score)
    kf = jnp.float32(top_k)

    def count_ge(cand):
        return jnp.sum((keys >= cand).astype(F32), axis=-1, keepdims=True)

    int_min = jnp.full((tq, 1), jnp.iinfo(jnp.int32).min, I32)
    thr = jnp.where(count_ge(jnp.zeros((tq, 1), I32)) >= kf, 0, int_min)

    def step(i, thr):
        cand = thr + jnp.left_shift(jnp.int32(1), 30 - i)
        return jnp.where(count_ge(cand) >= kf, cand, thr)

    thr = lax.fori_loop(0, 31, step, thr)

    gt = keys > thr
    eq = keys == thr
    need = kf - jnp.sum(gt.astype(F32), axis=-1, keepdims=True)
    tri = (lax.broadcasted_iota(I32, (LANES, LANES), 0) <= lax.broadcasted_iota(I32, (LANES, LANES), 1)).astype(BF16)
    run = jnp.zeros((tq, 1), F32)
    sel = []
    for c in range(S // LANES):
        sl = slice(c * LANES, (c + 1) * LANES)
        prefix = _dot(eq[:, sl].astype(BF16), tri) + run
        sel.append(gt[:, sl] | (eq[:, sl] & (prefix <= need)))
        run = prefix[:, LANES - 1:LANES]
    mask = jnp.concatenate(sel, axis=1) & allowed

    mask_g = jnp.concatenate([mask] * DSA_GROUP, axis=0)
    for g in range(DSA_KV_HEADS):
        kv = slice(g * DSA_HEAD_DIM, (g + 1) * DSA_HEAD_DIM)
        qg = jnp.concatenate([dq_ref[:, (g * DSA_GROUP + n) * DSA_HEAD_DIM:(g * DSA_GROUP + n + 1) * DSA_HEAD_DIM]
                              for n in range(DSA_GROUP)], axis=0)
        s = _dot_nt(qg, dk_ref[0:S, kv]) * scale
        s = jnp.where(mask_g, s, NEG_INF)
        p = jnp.exp(s - jnp.max(s, axis=-1, keepdims=True))
        l = jnp.sum(p, axis=-1, keepdims=True)
        o = _dot(p.astype(BF16), dv_ref[0:S, kv]) / l
        for n in range(DSA_GROUP):
            hs = (g * DSA_GROUP + n) * DSA_HEAD_DIM
            o_ref[:, hs:hs + DSA_HEAD_DIM] = o[n * tq:(n + 1) * tq].astype(BF16)


DSA_EXTENTS = 4


def _dsa_kernel(*refs, top_k, scale):
    S = refs[3].shape[0]
    nq = S // Q_BLOCK
    nvar = DSA_EXTENTS if nq % DSA_EXTENTS == 0 else 1
    per = nq // nvar
    for v in range(nvar):
        pl.when(pl.program_id(1) // per == v)(
            functools.partial(_dsa_body, *refs, (v + 1) * per * Q_BLOCK, top_k, scale))


def _dsa(iq, proj, dq, ka, kb, dk, dv, B, S):
    tq = Q_BLOCK
    nq = S // tq

    def qblk(width, off=0):
        return pl.BlockSpec((tq, width), lambda b, i: (b * nq + i, off // width))

    def kblk(width):
        return pl.BlockSpec((S, width), lambda b, i: (b, 0))

    return pl.pallas_call(
        functools.partial(_dsa_kernel, top_k=min(DSA_TOPK_MAX, S // 4), scale=DSA_HEAD_DIM ** -0.5),
        grid=(B, nq),
        in_specs=[qblk(1024), qblk(LANES, OFF_IKW), qblk(1024), kblk(LANES), kblk(LANES), kblk(256), kblk(256)],
        out_specs=qblk(1024),
        out_shape=jax.ShapeDtypeStruct((B * S, DSA_HEADS * DSA_HEAD_DIM), BF16),
        compiler_params=_params("parallel", "arbitrary"),
        name="dsa",
    )(iq, proj, dq, ka, kb, dk, dv)


def _merge_kernel(oa_ref, ob_ref, ga_ref, gb_ref, wa_ref, wb_ref, o_ref):
    a = _dot(oa_ref[...], wa_ref[...])
    b = _dot(ob_ref[...], wb_ref[...])
    o_ref[...] = (jax.nn.sigmoid(ga_ref[...]) * a + jax.nn.sigmoid(gb_ref[...]) * b).astype(BF16)


def _merge(oa, ob, proj, wa, wb, tm=256):
    T = oa.shape[0]
    row = lambda w, c=0: pl.BlockSpec((tm, w), lambda i: (i, c))
    full = lambda a: pl.BlockSpec(a.shape, lambda i: (0, 0))
    return pl.pallas_call(
        _merge_kernel,
        grid=(T // tm,),
        in_specs=[row(1024), row(1024), row(D_MODEL, 0), row(D_MODEL, 1), full(wa), full(wb)],
        out_specs=row(D_MODEL),
        out_shape=jax.ShapeDtypeStruct((T, D_MODEL), BF16),
        compiler_params=_params("parallel"),
        name="merge",
    )(oa, ob, proj, proj, wa, wb)


def _outproj_kernel(x_ref, m_ref, wo_ref, g_ref, wq_ref, x1_ref, h2_ref, qp_ref):
    x1 = x_ref[...] + _dot(m_ref[...], wo_ref[...])
    x1_ref[...] = x1
    h2 = _rms(x1, g_ref[...]).astype(BF16)
    h2_ref[...] = h2
    qp_ref[...] = _dot(h2, wq_ref[...]).astype(BF16)


def _outproj(x, merged, wo, g, wq, tm=256):
    T = x.shape[0]
    row = lambda w: pl.BlockSpec((tm, w), lambda i: (i, 0))
    full = lambda a: pl.BlockSpec(a.shape, lambda i: (0, 0))
    nq = wq.shape[1]
    return pl.pallas_call(
        _outproj_kernel,
        grid=(T // tm,),
        in_specs=[row(D_MODEL), row(D_MODEL), full(wo), full(g), full(wq)],
        out_specs=[row(D_MODEL), row(D_MODEL), row(nq)],
        out_shape=[jax.ShapeDtypeStruct((T, D_MODEL), F32), jax.ShapeDtypeStruct((T, D_MODEL), BF16),
                   jax.ShapeDtypeStruct((T, nq), BF16)],
        compiler_params=_params("parallel"),
        name="outproj",
    )(x, merged, wo, g, wq)


def _top16(cur, pos=None, payload=None):
    if pos is None:
        pos = lax.broadcasted_iota(I32, cur.shape, 0)
    big = jnp.iinfo(jnp.int32).max
    vals, idxs = [], []
    for _ in range(PEER_TOPK):
        m = jnp.max(cur, axis=0, keepdims=True)
        idx = jnp.min(jnp.where(cur == m, pos, big), axis=0, keepdims=True)
        hit = pos == idx
        vals.append(m)
        if payload is None:
            idxs.append(idx)
        else:
            idxs.append(jnp.max(jnp.where(hit, payload, -1), axis=0, keepdims=True))
        cur = jnp.where(hit, -jnp.inf, cur)
    return jnp.concatenate(vals, axis=0), jnp.concatenate(idxs, axis=0)


def _pair_candidates(s1, i1, s2, i2):
    tb = s1.shape[1]
    sub = lax.broadcasted_iota(I32, (8, tb), 0)
    cs, cp, ci = [s1[0:1] + s2], [lax.broadcasted_iota(I32, (PEER_TOPK, tb), 0)], [i1[0:1] * PEER_N_KEYS + i2]
    for a in range(1, 8):
        keep = sub < PEER_TOPK // (a + 1)
        cs.append(jnp.where(keep, s1[a:a + 1] + s2[0:8], -jnp.inf))
        cp.append(a * PEER_TOPK + sub)
        ci.append(i1[a:a + 1] * PEER_N_KEYS + i2[0:8])
    cs.append(s1[8:16] + s2[0:1])
    cp.append((8 + sub) * PEER_TOPK)
    ci.append(i1[8:16] * PEER_N_KEYS + i2[0:1])
    return jnp.concatenate(cs, axis=0), jnp.concatenate(cp, axis=0), jnp.concatenate(ci, axis=0)


def _route_kernel(qp_ref, keys_ref, idx_ref, g_ref):
    for h in range(PEER_HEADS):
        qpair = qp_ref[:, h * PEER_KEY_DIM:(h + 1) * PEER_KEY_DIM]
        s1, i1 = _top16(_dot_nt(keys_ref[2 * h], qpair))
        s2, i2 = _top16(_dot_nt(keys_ref[2 * h + 1], qpair))
        cand_s, cand_p, cand_i = _pair_candidates(s1, i1, s2, i2)
        top_s, expert = _top16(cand_s, cand_p, cand_i)
        e = jnp.exp(top_s - top_s[0:1])
        g = e / jnp.sum(e, axis=0, keepdims=True)
        idx_ref[0, h * PEER_TOPK:(h + 1) * PEER_TOPK, :] = expert
        g_ref[0, h * PEER_TOPK:(h + 1) * PEER_TOPK, :] = g


def _route(qp, keys_pad, tb=128):
    T = qp.shape[0]
    nb = T // tb
    nk = PEER_HEADS * PEER_TOPK
    out = pl.BlockSpec((1, nk, tb), lambda i: (i, 0, 0))
    return pl.pallas_call(
        _route_kernel,
        grid=(nb,),
        in_specs=[pl.BlockSpec((tb, qp.shape[1]), lambda i: (i, 0)),
                  pl.BlockSpec(keys_pad.shape, lambda i: (0, 0, 0))],
        out_specs=[out, out],
        out_shape=[jax.ShapeDtypeStruct((nb, nk, tb), I32), jax.ShapeDtypeStruct((nb, nk, tb), F32)],
        compiler_params=_params("parallel"),
        name="peer_route",
    )(qp, keys_pad)


PEER_TG = 8


def _peer_issue(idx_ref, t0, uv_hbm, buf, sem):
    nk = PEER_HEADS * PEER_TOPK
    for t in range(PEER_TG):
        for k in range(nk):
            pltpu.async_copy(uv_hbm.at[pl.ds(idx_ref[t0 + t, k], 1)], buf.at[pl.ds(t * nk + k, 1)], sem)


def _peer_wait(uv_hbm, buf, sem):
    pltpu.make_async_copy(uv_hbm.at[pl.ds(0, buf.shape[0])], buf, sem).wait()


def _peer_compute(buf, g, h, x1, gf):
    nk = PEER_HEADS * PEER_TOPK
    rows = PEER_TG * nk
    res = _dot_nt(h, buf[:, 0:D_MODEL].astype(BF16))
    row = lax.broadcasted_iota(I32, (PEER_TG, rows), 0)
    diag = (lax.broadcasted_iota(I32, (PEER_TG, rows), 1) // nk) == row
    res = jnp.where(diag, res, 0.0)
    a = res[:, 0:nk]
    for j in range(1, PEER_TG):
        a = a + res[:, j * nk:(j + 1) * nk]
    act = 0.5 * a * (1.0 + lax.erf(a * np.float32(2.0 ** -0.5)))
    coef = g * act
    coef_bd = jnp.where(diag, jnp.concatenate([coef] * PEER_TG, axis=1), 0.0).astype(BF16)
    out = _dot(coef_bd, buf[:, D_MODEL:2 * D_MODEL].astype(BF16))
    return _rms(x1 + out, gf)


def _peer_kernel(idx_ref, nxt_ref, g_ref, h_ref, x1_ref, gf_ref, uv_hbm, y_ref, buf_a, buf_b, sem):
    i = pl.program_id(0)
    lo, hi = slice(0, PEER_TG), slice(PEER_TG, 2 * PEER_TG)

    @pl.when(i == 0)
    def _():
        _peer_issue(idx_ref, 0, uv_hbm, buf_a, sem.at[0])

    _peer_wait(uv_hbm, buf_a, sem.at[0])
    _peer_issue(idx_ref, PEER_TG, uv_hbm, buf_b, sem.at[1])
    y_ref[lo, :] = _peer_compute(buf_a, g_ref[lo, :], h_ref[lo, :], x1_ref[lo, :], gf_ref[...])
    _peer_wait(uv_hbm, buf_b, sem.at[1])
    _peer_issue(nxt_ref, 0, uv_hbm, buf_a, sem.at[0])
    y_ref[hi, :] = _peer_compute(buf_b, g_ref[hi, :], h_ref[hi, :], x1_ref[hi, :], gf_ref[...])

    @pl.when(i == pl.num_programs(0) - 1)
    def _():
        _peer_wait(uv_hbm, buf_a, sem.at[0])


def _peer(idx, g, h2, x1, gf, uv):
    T = idx.shape[0]
    nk = idx.shape[1]
    ts = 2 * PEER_TG
    n = T // ts
    row = lambda w: pl.BlockSpec((ts, w), lambda i: (i, 0))
    return pl.pallas_call(
        _peer_kernel,
        grid=(n,),
        in_specs=[pl.BlockSpec((ts, nk), lambda i: (i, 0), memory_space=pltpu.SMEM),
                  pl.BlockSpec((ts, nk), lambda i: (jnp.minimum(i + 1, n - 1), 0), memory_space=pltpu.SMEM),
                  row(nk), row(D_MODEL), row(D_MODEL), pl.BlockSpec(gf.shape, lambda i: (0, 0)),
                  pl.BlockSpec(memory_space=pl.ANY)],
        out_specs=row(D_MODEL),
        out_shape=jax.ShapeDtypeStruct((T, D_MODEL), F32),
        scratch_shapes=[pltpu.VMEM((PEER_TG * nk, 2 * D_MODEL), F32), pltpu.VMEM((PEER_TG * nk, 2 * D_MODEL), F32),
                        pltpu.SemaphoreType.DMA((2,))],
        compiler_params=_params("arbitrary"),
        name="peer_ffn",
    )(idx, idx, g, h2, x1, gf, uv)


def _rope_tables(pos, rot, width):
    half = rot // 2
    inv_freq = ROPE_THETA ** (-(jnp.arange(half, dtype=F32) * 2.0) / rot)
    ang = pos.astype(F32)[:, None] * inv_freq
    cos, sin = jnp.cos(ang), jnp.sin(ang)
    T = pos.shape[0]
    one = jnp.ones((T, width - rot), F32)
    zero = jnp.zeros((T, width - rot), F32)
    zh = jnp.zeros((T, half), F32)
    c = jnp.concatenate([cos, cos, one], axis=1)
    sa = jnp.concatenate([zh, sin, zero], axis=1)
    sb = jnp.concatenate([-sin, zh, zero], axis=1)
    reps = LANES // width
    return jnp.stack([jnp.tile(t, (1, reps)) for t in (c, sa, sb)])


def kernel(x, positions, norm_mix_g, w_in, mla_q_norm_g, mla_kv_norm_g, mla_w_uq, mla_w_uk, mla_w_uv, w_branch_a, w_branch_b, w_out, norm_ffn_g, peer_w_q, peer_sub_keys, peer_u, peer_v, norm_final_g):
    B, S, D = x.shape
    T = B * S
    assert D == D_MODEL and w_in.shape[0] == 1 and S % 256 == 0 and T % 512 == 0
    xf = x.reshape(T, D)
    pos = positions.reshape(T)

    splits = [int(c) for c in np.cumsum(IN_SIZES)[:-1]]
    w_cq, w_ckv, w_kr, w_dq, w_dk, w_dv, w_iq, w_ik, w_iw, w_gates = jnp.split(w_in[0], splits, axis=1)
    zc = lambda n: jnp.zeros((D, n), F32)
    w_pack = jnp.concatenate([w_gates, w_dq, w_iq, w_cq, w_ckv, w_dk, w_dv, w_kr, zc(64), w_ik, w_iw, zc(48)],
                             axis=1).astype(BF16)
    wuq = jnp.pad(mla_w_uq[0], ((0, 0), (0, 0), (0, MLA_QK_PAD - MLA_NOPE - MLA_ROPE)))
    wuq = wuq.reshape(MLA_Q_RANK, MLA_HEADS * MLA_QK_PAD).astype(BF16)
    wuk = mla_w_uk[0].reshape(MLA_KV_RANK, MLA_HEADS * MLA_NOPE).astype(BF16)
    wuv = mla_w_uv[0].reshape(MLA_KV_RANK, MLA_HEADS * MLA_V).astype(BF16)
    wq_peer = peer_w_q[0].reshape(D, PEER_HEADS * PEER_KEY_DIM).astype(BF16)
    sk = peer_sub_keys[0].reshape(PEER_HEADS * 2, PEER_N_KEYS, PEER_KEY_DIM // 2)
    zk = jnp.zeros_like(sk)
    first = (jnp.arange(PEER_HEADS * 2) % 2 == 0)[:, None, None]
    keys_pad = jnp.where(first, jnp.concatenate([sk, zk], axis=2), jnp.concatenate([zk, sk], axis=2)).astype(BF16)

    tab_m = _rope_tables(pos, MLA_ROPE, LANES)
    tab_d = _rope_tables(pos, DSA_ROT, DSA_HEAD_DIM)
    tab_i = _rope_tables(pos, IDX_ROT, IDX_DIM)

    proj = _inproj(xf, norm_mix_g[0][None], w_pack)
    qm, km, vm, dq, dk, dv, iq, ka, kb = _prep(proj, tab_m, tab_d, tab_i, mla_q_norm_g[0][None],
                                               mla_kv_norm_g[0][None], wuq, wuk, wuv)
    o_a = _mla_attn(qm, km, vm, B, S)
    o_b = _dsa(iq, proj, dq, ka, kb, dk, dv, B, S)
    merged = _merge(o_a, o_b, proj, w_branch_a[0].astype(BF16), w_branch_b[0].astype(BF16))
    x1, h2, qp = _outproj(xf, merged, w_out[0].astype(BF16), norm_ffn_g[0][None], wq_peer)
    idx_t, g_t = _route(qp, keys_pad)
    nk = PEER_HEADS * PEER_TOPK
    idx = idx_t.transpose(0, 2, 1).reshape(T, nk)
    g = g_t.transpose(0, 2, 1).reshape(T, nk)
    uv = jnp.concatenate([peer_u[0], peer_v[0]], axis=1)
    y = _peer(idx, g, h2, x1, norm_final_g[None], uv)
    return y.reshape(B, S, D)
```

```python
import functools

import numpy as np
import jax
import jax.numpy as jnp
from jax import lax
from jax.experimental import pallas as pl
from jax.experimental.pallas import tpu as pltpu

F32 = jnp.float32
BF16 = jnp.bfloat16
I32 = jnp.int32

D_MODEL = 2048
CHUNK = 64
Q_BLOCK = 128
ROPE_THETA = 500000.0
EPS = 1e-6
NEG_INF = -1e30

MLA_HEADS = 8
MLA_NOPE = 128
MLA_ROPE = 64
MLA_V = 128
MLA_Q_RANK = 512
MLA_KV_RANK = 256
MLA_QK_PAD = 256

DSA_HEADS = 8
DSA_KV_HEADS = 2
DSA_GROUP = DSA_HEADS // DSA_KV_HEADS
DSA_HEAD_DIM = 128
DSA_ROT = DSA_HEAD_DIM // 4
IDX_HEADS = 16
IDX_DIM = 64
IDX_ROT = IDX_DIM // 4
DSA_TOPK_MAX = 256

PEER_HEADS = 8
PEER_N_KEYS = 128
PEER_KEY_DIM = 128
PEER_TOPK = 16

IN_SIZES = (MLA_Q_RANK, MLA_KV_RANK, MLA_ROPE, DSA_HEADS * DSA_HEAD_DIM, DSA_KV_HEADS * DSA_HEAD_DIM,
            DSA_KV_HEADS * DSA_HEAD_DIM, IDX_HEADS * IDX_DIM, IDX_DIM, IDX_HEADS, 2 * D_MODEL)

LANES = 128

OFF_GATES = 0
OFF_DQ = 4096
OFF_IQ = 5120
OFF_CQ = 6144
OFF_CKV = 6656
OFF_DK = 6912
OFF_DV = 7168
OFF_KROPE = 7424
OFF_IKW = 7552
N_PACK = 7680

VMEM_LIMIT = 56 * 1024 * 1024


def _params(*sem):
    return pltpu.CompilerParams(dimension_semantics=sem, vmem_limit_bytes=VMEM_LIMIT)


def _rms(t, g):
    return t * lax.rsqrt(jnp.mean(t * t, axis=-1, keepdims=True) + EPS) * g


def _dot(a, b):
    return jnp.dot(a, b, preferred_element_type=F32)


def _dot_nt(a, b):
    return lax.dot_general(a, b, (((1,), (1,)), ((), ())), preferred_element_type=F32)


def _rope(t, c, sa, sb, half):
    return t * c + pltpu.roll(t, half, 1) * sa + pltpu.roll(t, LANES - half, 1) * sb


def _inproj_kernel(x_ref, g_ref, w_ref, o_ref, h_ref):
    @pl.when(pl.program_id(1) == 0)
    def _():
        h_ref[...] = _rms(x_ref[...], g_ref[...]).astype(BF16)

    o_ref[...] = _dot(h_ref[...], w_ref[...])


def _inproj(x, g, w, tm=512, tn=1920):
    T = x.shape[0]
    return pl.pallas_call(
        _inproj_kernel,
        grid=(T // tm, N_PACK // tn),
        in_specs=[pl.BlockSpec((tm, D_MODEL), lambda i, j: (i, 0)),
                  pl.BlockSpec((1, D_MODEL), lambda i, j: (0, 0)),
                  pl.BlockSpec((D_MODEL, tn), lambda i, j: (0, j))],
        out_specs=pl.BlockSpec((tm, tn), lambda i, j: (i, j)),
        out_shape=jax.ShapeDtypeStruct((T, N_PACK), F32),
        scratch_shapes=[pltpu.VMEM((tm, D_MODEL), BF16)],
        compiler_params=_params("parallel", "arbitrary"),
        name="inproj",
    )(x, g, w)


def _prep_kernel(cq_ref, ckv_ref, kr_ref, dq_ref, dk_ref, dv_ref, iq_ref, ikw_ref,
                 tm_ref, td_ref, ti_ref, gq_ref, gkv_ref, wuq_ref, wuk_ref, wuv_ref,
                 qm_ref, km_ref, vm_ref, dqo_ref, dko_ref, dvo_ref, iqo_ref, ka_ref, kb_ref):
    cm, sam, sbm = tm_ref[0], tm_ref[1], tm_ref[2]
    cd, sad, sbd = td_ref[0], td_ref[1], td_ref[2]
    ci, sai, sbi = ti_ref[0], ti_ref[1], ti_ref[2]

    qn = _rms(cq_ref[...], gq_ref[...]).astype(BF16)
    q = _dot(qn, wuq_ref[...])
    for h in range(MLA_HEADS):
        lo = h * MLA_QK_PAD
        qm_ref[:, lo:lo + LANES] = q[:, lo:lo + LANES].astype(BF16)
        qm_ref[:, lo + LANES:lo + 2 * LANES] = _rope(q[:, lo + LANES:lo + 2 * LANES], cm, sam, sbm,
                                                       MLA_ROPE // 2).astype(BF16)
    ckv = _rms(ckv_ref[...], gkv_ref[...]).astype(BF16)
    kn = _dot(ckv, wuk_ref[...])
    kpe = _rope(kr_ref[...], cm, sam, sbm, MLA_ROPE // 2).astype(BF16)
    for h in range(MLA_HEADS):
        lo = h * MLA_QK_PAD
        km_ref[:, lo:lo + LANES] = kn[:, h * MLA_NOPE:(h + 1) * MLA_NOPE].astype(BF16)
        km_ref[:, lo + LANES:lo + 2 * LANES] = kpe
    vm_ref[...] = _dot(ckv, wuv_ref[...]).astype(BF16)

    for h in range(DSA_HEADS):
        sl = slice(h * LANES, (h + 1) * LANES)
        dqo_ref[:, sl] = _rope(dq_ref[:, sl], cd, sad, sbd, DSA_ROT // 2).astype(BF16)
    for h in range(DSA_KV_HEADS):
        sl = slice(h * LANES, (h + 1) * LANES)
        dko_ref[:, sl] = _rope(dk_ref[:, sl], cd, sad, sbd, DSA_ROT // 2).astype(BF16)
    dvo_ref[...] = dv_ref[...].astype(BF16)

    for c in range(IDX_HEADS * IDX_DIM // LANES):
        sl = slice(c * LANES, (c + 1) * LANES)
        iqo_ref[:, sl] = _rope(iq_ref[:, sl], ci, sai, sbi, IDX_ROT // 2).astype(BF16)
    lane = lax.broadcasted_iota(I32, ikw_ref.shape, 1)
    ka = jnp.where(lane < IDX_DIM, _rope(ikw_ref[...], ci, sai, sbi, IDX_ROT // 2), 0.0)
    ka_ref[...] = ka.astype(BF16)
    kb_ref[...] = pltpu.roll(ka, IDX_DIM, 1).astype(BF16)


def _prep(proj, tab_m, tab_d, tab_i, gq, gkv, wuq, wuk, wuv, tm=256):
    T = proj.shape[0]

    def col(width, off):
        return pl.BlockSpec((tm, width), lambda i: (i, off // width))

    def full(a):
        return pl.BlockSpec(a.shape, lambda i: (0,) * a.ndim)

    tab = pl.BlockSpec((3, tm, LANES), lambda i: (0, i, 0))
    widths = (MLA_HEADS * MLA_QK_PAD, MLA_HEADS * MLA_QK_PAD, MLA_HEADS * MLA_V, DSA_HEADS * DSA_HEAD_DIM,
              DSA_KV_HEADS * DSA_HEAD_DIM, DSA_KV_HEADS * DSA_HEAD_DIM, IDX_HEADS * IDX_DIM, LANES, LANES)
    return pl.pallas_call(
        _prep_kernel,
        grid=(T // tm,),
        in_specs=[col(MLA_Q_RANK, OFF_CQ), col(MLA_KV_RANK, OFF_CKV), col(LANES, OFF_KROPE),
                  col(1024, OFF_DQ), col(256, OFF_DK), col(256, OFF_DV), col(1024, OFF_IQ), col(LANES, OFF_IKW),
                  tab, tab, tab, full(gq), full(gkv), full(wuq), full(wuk), full(wuv)],
        out_specs=[pl.BlockSpec((tm, w), lambda i: (i, 0)) for w in widths],
        out_shape=[jax.ShapeDtypeStruct((T, w), BF16) for w in widths],
        compiler_params=_params("parallel"),
        name="prep",
    )(proj, proj, proj, proj, proj, proj, proj, proj, tab_m, tab_d, tab_i, gq, gkv, wuq, wuk, wuv)


def _mla_attn_tile(q_ref, k_ref, v_ref, o_ref, c, tq, scale):
    n = (c + 1) * tq
    q_chunk = (c * tq + lax.broadcasted_iota(I32, (tq, n), 0)) // CHUNK
    k_chunk = lax.broadcasted_iota(I32, (tq, n), 1) // CHUNK
    allowed = k_chunk <= q_chunk
    for h in range(MLA_HEADS):
        qk = slice(h * MLA_QK_PAD, (h + 1) * MLA_QK_PAD)
        s = _dot_nt(q_ref[:, qk], k_ref[0:n, qk]) * scale
        s = jnp.where(allowed, s, NEG_INF)
        p = jnp.exp(s - jnp.max(s, axis=-1, keepdims=True))
        l = jnp.sum(p, axis=-1, keepdims=True)
        vs = slice(h * MLA_V, (h + 1) * MLA_V)
        o_ref[:, vs] = (_dot(p.astype(BF16), v_ref[0:n, vs]) / l).astype(BF16)


def _mla_attn_kernel(q_ref, k_ref, v_ref, o_ref, *, tq, scale):
    for c in range(k_ref.shape[0] // tq):
        pl.when(pl.program_id(1) == c)(functools.partial(_mla_attn_tile, q_ref, k_ref, v_ref, o_ref, c, tq, scale))


def _mla_attn(q, k, v, B, S, tq=256):
    nq = S // tq
    return pl.pallas_call(
        functools.partial(_mla_attn_kernel, tq=tq, scale=(MLA_NOPE + MLA_ROPE) ** -0.5),
        grid=(B, nq),
        in_specs=[pl.BlockSpec((tq, q.shape[1]), lambda b, i: (b * nq + i, 0)),
                  pl.BlockSpec((S, k.shape[1]), lambda b, i: (b, 0)),
                  pl.BlockSpec((S, v.shape[1]), lambda b, i: (b, 0))],
        out_specs=pl.BlockSpec((tq, v.shape[1]), lambda b, i: (b * nq + i, 0)),
        out_shape=jax.ShapeDtypeStruct((B * S, v.shape[1]), BF16),
        compiler_params=_params("parallel", "arbitrary"),
        name="mla_attn",
    )(q, k, v)


def _order_key(score):
    bits = pltpu.bitcast(score + 0.0, I32)
    return bits ^ ((bits >> 31) & jnp.int32(0x7FFFFFFF))


def _dsa_body(iq_ref, ikw_ref, dq_ref, ka_ref, kb_ref, dk_ref, dv_ref, o_ref, S, top_k, scale):
    tq = Q_BLOCK
    q0 = pl.program_id(1) * tq
    q_chunk = (q0 + lax.broadcasted_iota(I32, (tq, S), 0)) // CHUNK
    k_chunk = lax.broadcasted_iota(I32, (tq, S), 1) // CHUNK
    allowed = k_chunk <= q_chunk

    w = ikw_ref[...] * (IDX_HEADS ** -0.5)
    score = jnp.zeros((tq, S), F32)
    for c in range(IDX_HEADS // 2):
        qpair = iq_ref[:, c * LANES:(c + 1) * LANES]
        for half, k_ref in enumerate((ka_ref, kb_ref)):
            h = 2 * c + half
            dots = _dot_nt(qpair, k_ref[0:S, :]) * (IDX_DIM ** -0.5)
            score = score + w[:, IDX_DIM + h:IDX_DIM + h + 1] * jnp.maximum(dots, 0.0)
    score = jnp.where(allowed, score, NEG_INF)

    keys = _order_key(score)
    kf = jnp.float32(top_k)

    def count_ge(cand):
        return jnp.sum((keys >= cand).astype(F32), axis=-1, keepdims=True)

    int_min = jnp.full((tq, 1), jnp.iinfo(jnp.int32).min, I32)
    thr = jnp.where(count_ge(jnp.zeros((tq, 1), I32)) >= kf, 0, int_min)

    def step(i, thr):
        cand = thr + jnp.left_shift(jnp.int32(1), 30 - i)
        return jnp.where(count_ge(cand) >= kf, cand, thr)

    thr = lax.fori_loop(0, 31, step, thr)

    gt = keys > thr
    eq = keys == thr
    need = kf - jnp.sum(gt.astype(F32), axis=-1, keepdims=True)
    tri = (lax.broadcasted_iota(I32, (LANES, LANES), 0) <= lax.broadcasted_iota(I32, (LANES, LANES), 1)).astype(BF16)
    run = jnp.zeros((tq, 1), F32)
    sel = []
    for c in range(S // LANES):
        sl = slice(c * LANES, (c + 1) * LANES)
        prefix = _dot(eq[:, sl].astype(BF16), tri) + run
        sel.append(gt[:, sl] | (eq[:, sl] & (prefix <= need)))
        run = prefix[:, LANES - 1:LANES]
    mask = jnp.concatenate(sel, axis=1) & allowed

    mask_g = jnp.concatenate([mask] * DSA_GROUP, axis=0)
    for g in range(DSA_KV_HEADS):
        kv = slice(g * DSA_HEAD_DIM, (g + 1) * DSA_HEAD_DIM)
        qg = jnp.concatenate([dq_ref[:, (g * DSA_GROUP + n) * DSA_HEAD_DIM:(g * DSA_GROUP + n + 1) * DSA_HEAD_DIM]
                              for n in range(DSA_GROUP)], axis=0)
        s = _dot_nt(qg, dk_ref[0:S, kv]) * scale
        s = jnp.where(mask_g, s, NEG_INF)
        p = jnp.exp(s - jnp.max(s, axis=-1, keepdims=True))
        l = jnp.sum(p, axis=-1, keepdims=True)
        o = _dot(p.astype(BF16), dv_ref[0:S, kv]) / l
        for n in range(DSA_GROUP):
            hs = (g * DSA_GROUP + n) * DSA_HEAD_DIM
            o_ref[:, hs:hs + DSA_HEAD_DIM] = o[n * tq:(n + 1) * tq].astype(BF16)


DSA_EXTENTS = 4


def _dsa_kernel(*refs, top_k, scale):
    S = refs[3].shape[0]
    nq = S // Q_BLOCK
    nvar = DSA_EXTENTS if nq % DSA_EXTENTS == 0 else 1
    per = nq // nvar
    for v in range(nvar):
        pl.when(pl.program_id(1) // per == v)(
            functools.partial(_dsa_body, *refs, (v + 1) * per * Q_BLOCK, top_k, scale))


def _dsa(iq, proj, dq, ka, kb, dk, dv, B, S):
    tq = Q_BLOCK
    nq = S // tq

    def qblk(width, off=0):
        return pl.BlockSpec((tq, width), lambda b, i: (b * nq + i, off // width))

    def kblk(width):
        return pl.BlockSpec((S, width), lambda b, i: (b, 0))

    return pl.pallas_call(
        functools.partial(_dsa_kernel, top_k=min(DSA_TOPK_MAX, S // 4), scale=DSA_HEAD_DIM ** -0.5),
        grid=(B, nq),
        in_specs=[qblk(1024), qblk(LANES, OFF_IKW), qblk(1024), kblk(LANES), kblk(LANES), kblk(256), kblk(256)],
        out_specs=qblk(1024),
        out_shape=jax.ShapeDtypeStruct((B * S, DSA_HEADS * DSA_HEAD_DIM), BF16),
        compiler_params=_params("parallel", "arbitrary"),
        name="dsa",
    )(iq, proj, dq, ka, kb, dk, dv)


def _merge_kernel(oa_ref, ob_ref, ga_ref, gb_ref, wa_ref, wb_ref, o_ref):
    a = _dot(oa_ref[...], wa_ref[...])
    b = _dot(ob_ref[...], wb_ref[...])
    o_ref[...] = (jax.nn.sigmoid(ga_ref[...]) * a + jax.nn.sigmoid(gb_ref[...]) * b).astype(BF16)


def _merge(oa, ob, proj, wa, wb, tm=256):
    T = oa.shape[0]
    row = lambda w, c=0: pl.BlockSpec((tm, w), lambda i: (i, c))
    full = lambda a: pl.BlockSpec(a.shape, lambda i: (0, 0))
    return pl.pallas_call(
        _merge_kernel,
        grid=(T // tm,),
        in_specs=[row(1024), row(1024), row(D_MODEL, 0), row(D_MODEL, 1), full(wa), full(wb)],
        out_specs=row(D_MODEL),
        out_shape=jax.ShapeDtypeStruct((T, D_MODEL), BF16),
        compiler_params=_params("parallel"),
        name="merge",
    )(oa, ob, proj, proj, wa, wb)


def _outproj_kernel(x_ref, m_ref, wo_ref, g_ref, wq_ref, x1_ref, h2_ref, qp_ref):
    x1 = x_ref[...] + _dot(m_ref[...], wo_ref[...])
    x1_ref[...] = x1
    h2 = _rms(x1, g_ref[...]).astype(BF16)
    h2_ref[...] = h2
    qp_ref[...] = _dot(h2, wq_ref[...]).astype(BF16)


def _outproj(x, merged, wo, g, wq, tm=256):
    T = x.shape[0]
    row = lambda w: pl.BlockSpec((tm, w), lambda i: (i, 0))
    full = lambda a: pl.BlockSpec(a.shape, lambda i: (0, 0))
    nq = wq.shape[1]
    return pl.pallas_call(
        _outproj_kernel,
        grid=(T // tm,),
        in_specs=[row(D_MODEL), row(D_MODEL), full(wo), full(g), full(wq)],
        out_specs=[row(D_MODEL), row(D_MODEL), row(nq)],
        out_shape=[jax.ShapeDtypeStruct((T, D_MODEL), F32), jax.ShapeDtypeStruct((T, D_MODEL), BF16),
                   jax.ShapeDtypeStruct((T, nq), BF16)],
        compiler_params=_params("parallel"),
        name="outproj",
    )(x, merged, wo, g, wq)


def _top16(cur, pos=None, payload=None):
    if pos is None:
        pos = lax.broadcasted_iota(I32, cur.shape, 0)
    big = jnp.iinfo(jnp.int32).max
    vals, idxs = [], []
    for _ in range(PEER_TOPK):
        m = jnp.max(cur, axis=0, keepdims=True)
        idx = jnp.min(jnp.where(cur == m, pos, big), axis=0, keepdims=True)
        hit = pos == idx
        vals.append(m)
        if payload is None:
            idxs.append(idx)
        else:
            idxs.append(jnp.max(jnp.where(hit, payload, -1), axis=0, keepdims=True))
        cur = jnp.where(hit, -jnp.inf, cur)
    return jnp.concatenate(vals, axis=0), jnp.concatenate(idxs, axis=0)


def _pair_candidates(s1, i1, s2, i2):
    tb = s1.shape[1]
    sub = lax.broadcasted_iota(I32, (8, tb), 0)
    cs, cp, ci = [s1[0:1] + s2], [lax.broadcasted_iota(I32, (PEER_TOPK, tb), 0)], [i1[0:1] * PEER_N_KEYS + i2]
    for a in range(1, 8):
        keep = sub < PEER_TOPK // (a + 1)
        cs.append(jnp.where(keep, s1[a:a + 1] + s2[0:8], -jnp.inf))
        cp.append(a * PEER_TOPK + sub)
        ci.append(i1[a:a + 1] * PEER_N_KEYS + i2[0:8])
    cs.append(s1[8:16] + s2[0:1])
    cp.append((8 + sub) * PEER_TOPK)
    ci.append(i1[8:16] * PEER_N_KEYS + i2[0:1])
    return jnp.concatenate(cs, axis=0), jnp.concatenate(cp, axis=0), jnp.concatenate(ci, axis=0)


def _route_kernel(qp_ref, keys_ref, idx_ref, g_ref):
    for h in range(PEER_HEADS):
        qpair = qp_ref[:, h * PEER_KEY_DIM:(h + 1) * PEER_KEY_DIM]
        s1, i1 = _top16(_dot_nt(keys_ref[2 * h], qpair))
        s2, i2 = _top16(_dot_nt(keys_ref[2 * h + 1], qpair))
        cand_s, cand_p, cand_i = _pair_candidates(s1, i1, s2, i2)
        top_s, expert = _top16(cand_s, cand_p, cand_i)
        e = jnp.exp(top_s - top_s[0:1])
        g = e / jnp.sum(e, axis=0, keepdims=True)
        idx_ref[0, h * PEER_TOPK:(h + 1) * PEER_TOPK, :] = expert
        g_ref[0, h * PEER_TOPK:(h + 1) * PEER_TOPK, :] = g


def _route(qp, keys_pad, tb=128):
    T = qp.shape[0]
    nb = T // tb
    nk = PEER_HEADS * PEER_TOPK
    out = pl.BlockSpec((1, nk, tb), lambda i: (i, 0, 0))
    return pl.pallas_call(
        _route_kernel,
        grid=(nb,),
        in_specs=[pl.BlockSpec((tb, qp.shape[1]), lambda i: (i, 0)),
                  pl.BlockSpec(keys_pad.shape, lambda i: (0, 0, 0))],
        out_specs=[out, out],
        out_shape=[jax.ShapeDtypeStruct((nb, nk, tb), I32), jax.ShapeDtypeStruct((nb, nk, tb), F32)],
        compiler_params=_params("parallel"),
        name="peer_route",
    )(qp, keys_pad)


PEER_TG = 8


PEER_EARLY = 4


def _peer_issue(idx_ref, t0, uv_hbm, buf, sem, first=0, last=PEER_TG):
    nk = PEER_HEADS * PEER_TOPK
    for t in range(first, last):
        for k in range(nk):
            pltpu.async_copy(uv_hbm.at[pl.ds(idx_ref[t0 + t, k], 1)], buf.at[pl.ds(t * nk + k, 1)], sem)


def _peer_wait(uv_hbm, buf, sem):
    pltpu.make_async_copy(uv_hbm.at[pl.ds(0, buf.shape[0])], buf, sem).wait()


def _peer_compute(buf, g, h, x1, gf):
    nk = PEER_HEADS * PEER_TOPK
    rows = PEER_TG * nk
    w = buf[...]
    ub = pltpu.bitcast(w << 16, F32).astype(BF16)
    vb = pltpu.bitcast(w & jnp.int32(-65536), F32).astype(BF16)
    res = _dot_nt(h, ub)
    row = lax.broadcasted_iota(I32, (PEER_TG, rows), 0)
    diag = (lax.broadcasted_iota(I32, (PEER_TG, rows), 1) // nk) == row
    res = jnp.where(diag, res, 0.0)
    a = res[:, 0:nk]
    for j in range(1, PEER_TG):
        a = a + res[:, j * nk:(j + 1) * nk]
    act = 0.5 * a * (1.0 + lax.erf(a * np.float32(2.0 ** -0.5)))
    coef = g * act
    coef_bd = jnp.where(diag, jnp.concatenate([coef] * PEER_TG, axis=1), 0.0).astype(BF16)
    out = _dot(coef_bd, vb)
    return _rms(x1 + out, gf)


def _peer_kernel(idx_ref, nxt_ref, g_ref, h_ref, x1_ref, gf_ref, uv_hbm, y_ref, buf_a, buf_b, sem):
    i = pl.program_id(0)
    lo, hi = slice(0, PEER_TG), slice(PEER_TG, 2 * PEER_TG)

    @pl.when(i == 0)
    def _():
        _peer_issue(idx_ref, 0, uv_hbm, buf_a, sem.at[0])

    _peer_issue(idx_ref, PEER_TG, uv_hbm, buf_b, sem.at[1], 0, PEER_EARLY)
    _peer_wait(uv_hbm, buf_a, sem.at[0])
    _peer_issue(idx_ref, PEER_TG, uv_hbm, buf_b, sem.at[1], PEER_EARLY, PEER_TG)
    y_ref[lo, :] = _peer_compute(buf_a, g_ref[lo, :], h_ref[lo, :], x1_ref[lo, :], gf_ref[...])
    _peer_issue(nxt_ref, 0, uv_hbm, buf_a, sem.at[0], 0, PEER_EARLY)
    _peer_wait(uv_hbm, buf_b, sem.at[1])
    _peer_issue(nxt_ref, 0, uv_hbm, buf_a, sem.at[0], PEER_EARLY, PEER_TG)
    y_ref[hi, :] = _peer_compute(buf_b, g_ref[hi, :], h_ref[hi, :], x1_ref[hi, :], gf_ref[...])

    @pl.when(i == pl.num_programs(0) - 1)
    def _():
        _peer_wait(uv_hbm, buf_a, sem.at[0])


def _peer(idx, g, h2, x1, gf, uv):
    T = idx.shape[0]
    nk = idx.shape[1]
    ts = 2 * PEER_TG
    n = T // ts
    row = lambda w: pl.BlockSpec((ts, w), lambda i: (i, 0))
    return pl.pallas_call(
        _peer_kernel,
        grid=(n,),
        in_specs=[pl.BlockSpec((ts, nk), lambda i: (i, 0), memory_space=pltpu.SMEM),
                  pl.BlockSpec((ts, nk), lambda i: (jnp.minimum(i + 1, n - 1), 0), memory_space=pltpu.SMEM),
                  row(nk), row(D_MODEL), row(D_MODEL), pl.BlockSpec(gf.shape, lambda i: (0, 0)),
                  pl.BlockSpec(memory_space=pl.ANY)],
        out_specs=row(D_MODEL),
        out_shape=jax.ShapeDtypeStruct((T, D_MODEL), F32),
        scratch_shapes=[pltpu.VMEM((PEER_TG * nk, D_MODEL), I32), pltpu.VMEM((PEER_TG * nk, D_MODEL), I32),
                        pltpu.SemaphoreType.DMA((2,))],
        compiler_params=_params("arbitrary"),
        name="peer_ffn",
    )(idx, idx, g, h2, x1, gf, uv)


def _rope_tables(pos, rot, width):
    half = rot // 2
    inv_freq = ROPE_THETA ** (-(jnp.arange(half, dtype=F32) * 2.0) / rot)
    ang = pos.astype(F32)[:, None] * inv_freq
    cos, sin = jnp.cos(ang), jnp.sin(ang)
    T = pos.shape[0]
    one = jnp.ones((T, width - rot), F32)
    zero = jnp.zeros((T, width - rot), F32)
    zh = jnp.zeros((T, half), F32)
    c = jnp.concatenate([cos, cos, one], axis=1)
    sa = jnp.concatenate([zh, sin, zero], axis=1)
    sb = jnp.concatenate([-sin, zh, zero], axis=1)
    reps = LANES // width
    return jnp.stack([jnp.tile(t, (1, reps)) for t in (c, sa, sb)])


def kernel(x, positions, norm_mix_g, w_in, mla_q_norm_g, mla_kv_norm_g, mla_w_uq, mla_w_uk, mla_w_uv, w_branch_a, w_branch_b, w_out, norm_ffn_g, peer_w_q, peer_sub_keys, peer_u, peer_v, norm_final_g):
    B, S, D = x.shape
    T = B * S
    assert D == D_MODEL and w_in.shape[0] == 1 and S % 256 == 0 and T % 512 == 0
    xf = x.reshape(T, D)
    pos = positions.reshape(T)

    splits = [int(c) for c in np.cumsum(IN_SIZES)[:-1]]
    w_cq, w_ckv, w_kr, w_dq, w_dk, w_dv, w_iq, w_ik, w_iw, w_gates = jnp.split(w_in[0], splits, axis=1)
    zc = lambda n: jnp.zeros((D, n), F32)
    w_pack = jnp.concatenate([w_gates, w_dq, w_iq, w_cq, w_ckv, w_dk, w_dv, w_kr, zc(64), w_ik, w_iw, zc(48)],
                             axis=1).astype(BF16)
    wuq = jnp.pad(mla_w_uq[0], ((0, 0), (0, 0), (0, MLA_QK_PAD - MLA_NOPE - MLA_ROPE)))
    wuq = wuq.reshape(MLA_Q_RANK, MLA_HEADS * MLA_QK_PAD).astype(BF16)
    wuk = mla_w_uk[0].reshape(MLA_KV_RANK, MLA_HEADS * MLA_NOPE).astype(BF16)
    wuv = mla_w_uv[0].reshape(MLA_KV_RANK, MLA_HEADS * MLA_V).astype(BF16)
    wq_peer = peer_w_q[0].reshape(D, PEER_HEADS * PEER_KEY_DIM).astype(BF16)
    sk = peer_sub_keys[0].reshape(PEER_HEADS * 2, PEER_N_KEYS, PEER_KEY_DIM // 2)
    zk = jnp.zeros_like(sk)
    first = (jnp.arange(PEER_HEADS * 2) % 2 == 0)[:, None, None]
    keys_pad = jnp.where(first, jnp.concatenate([sk, zk], axis=2), jnp.concatenate([zk, sk], axis=2)).astype(BF16)

    tab_m = _rope_tables(pos, MLA_ROPE, LANES)
    tab_d = _rope_tables(pos, DSA_ROT, DSA_HEAD_DIM)
    tab_i = _rope_tables(pos, IDX_ROT, IDX_DIM)

    proj = _inproj(xf, norm_mix_g[0][None], w_pack)
    qm, km, vm, dq, dk, dv, iq, ka, kb = _prep(proj, tab_m, tab_d, tab_i, mla_q_norm_g[0][None],
                                               mla_kv_norm_g[0][None], wuq, wuk, wuv)
    o_a = _mla_attn(qm, km, vm, B, S)
    o_b = _dsa(iq, proj, dq, ka, kb, dk, dv, B, S)
    merged = _merge(o_a, o_b, proj, w_branch_a[0].astype(BF16), w_branch_b[0].astype(BF16))
    x1, h2, qp = _outproj(xf, merged, w_out[0].astype(BF16), norm_ffn_g[0][None], wq_peer)
    idx_t, g_t = _route(qp, keys_pad)
    nk = PEER_HEADS * PEER_TOPK
    idx = idx_t.transpose(0, 2, 1).reshape(T, nk)
    g = g_t.transpose(0, 2, 1).reshape(T, nk)
    half = lambda t: lax.bitcast_convert_type(t.astype(BF16), jnp.uint16).astype(jnp.uint32)
    uv = lax.bitcast_convert_type(half(peer_u[0]) | (half(peer_v[0]) << 16), I32)
    y = _peer(idx, g, h2, x1, norm_final_g[None], uv)
    return y.reshape(B, S, D)
```

```python
import functools

import numpy as np
import jax
import jax.numpy as jnp
from jax import lax
from jax.experimental import pallas as pl
from jax.experimental.pallas import tpu as pltpu

F32 = jnp.float32
BF16 = jnp.bfloat16
I32 = jnp.int32

D_MODEL = 2048
CHUNK = 64
Q_BLOCK = 128
ROPE_THETA = 500000.0
EPS = 1e-6
NEG_INF = -1e30

MLA_HEADS = 8
MLA_NOPE = 128
MLA_ROPE = 64
MLA_V = 128
MLA_Q_RANK = 512
MLA_KV_RANK = 256
MLA_QK_PAD = 256

DSA_HEADS = 8
DSA_KV_HEADS = 2
DSA_GROUP = DSA_HEADS // DSA_KV_HEADS
DSA_HEAD_DIM = 128
DSA_ROT = DSA_HEAD_DIM // 4
IDX_HEADS = 16
IDX_DIM = 64
IDX_ROT = IDX_DIM // 4
DSA_TOPK_MAX = 256

PEER_HEADS = 8
PEER_N_KEYS = 128
PEER_KEY_DIM = 128
PEER_TOPK = 16

IN_SIZES = (MLA_Q_RANK, MLA_KV_RANK, MLA_ROPE, DSA_HEADS * DSA_HEAD_DIM, DSA_KV_HEADS * DSA_HEAD_DIM,
            DSA_KV_HEADS * DSA_HEAD_DIM, IDX_HEADS * IDX_DIM, IDX_DIM, IDX_HEADS, 2 * D_MODEL)

LANES = 128

OFF_GATES = 0
OFF_DQ = 4096
OFF_IQ = 5120
OFF_CQ = 6144
OFF_CKV = 6656
OFF_DK = 6912
OFF_DV = 7168
OFF_KROPE = 7424
OFF_IKW = 7552
N_PACK = 7680

VMEM_LIMIT = 56 * 1024 * 1024


def _params(*sem):
    return pltpu.CompilerParams(dimension_semantics=sem, vmem_limit_bytes=VMEM_LIMIT)


def _rms(t, g):
    return t * lax.rsqrt(jnp.mean(t * t, axis=-1, keepdims=True) + EPS) * g


def _dot(a, b):
    return jnp.dot(a, b, preferred_element_type=F32)


def _dot_nt(a, b):
    return lax.dot_general(a, b, (((1,), (1,)), ((), ())), preferred_element_type=F32)


def _rope(t, c, sa, sb, half):
    return t * c + pltpu.roll(t, half, 1) * sa + pltpu.roll(t, LANES - half, 1) * sb


def _inproj_kernel(x_ref, g_ref, w_ref, o_ref, h_ref):
    @pl.when(pl.program_id(1) == 0)
    def _():
        h_ref[...] = _rms(x_ref[...], g_ref[...]).astype(BF16)

    o_ref[...] = _dot(h_ref[...], w_ref[...])


def _inproj(x, g, w, tm=512, tn=1920):
    T = x.shape[0]
    return pl.pallas_call(
        _inproj_kernel,
        grid=(T // tm, N_PACK // tn),
        in_specs=[pl.BlockSpec((tm, D_MODEL), lambda i, j: (i, 0)),
                  pl.BlockSpec((1, D_MODEL), lambda i, j: (0, 0)),
                  pl.BlockSpec((D_MODEL, tn), lambda i, j: (0, j))],
        out_specs=pl.BlockSpec((tm, tn), lambda i, j: (i, j)),
        out_shape=jax.ShapeDtypeStruct((T, N_PACK), F32),
        scratch_shapes=[pltpu.VMEM((tm, D_MODEL), BF16)],
        compiler_params=_params("parallel", "arbitrary"),
        name="inproj",
    )(x, g, w)


def _prep_kernel(cq_ref, ckv_ref, kr_ref, dq_ref, dk_ref, dv_ref, iq_ref, ikw_ref,
                 tm_ref, td_ref, ti_ref, gq_ref, gkv_ref, wuq_ref, wuk_ref, wuv_ref,
                 qm_ref, km_ref, vm_ref, dqo_ref, dko_ref, dvo_ref, iqo_ref, ka_ref, kb_ref):
    cm, sam, sbm = tm_ref[0], tm_ref[1], tm_ref[2]
    cd, sad, sbd = td_ref[0], td_ref[1], td_ref[2]
    ci, sai, sbi = ti_ref[0], ti_ref[1], ti_ref[2]

    qn = _rms(cq_ref[...], gq_ref[...]).astype(BF16)
    q = _dot(qn, wuq_ref[...])
    for h in range(MLA_HEADS):
        lo = h * MLA_QK_PAD
        qm_ref[:, lo:lo + LANES] = q[:, lo:lo + LANES].astype(BF16)
        qm_ref[:, lo + LANES:lo + 2 * LANES] = _rope(q[:, lo + LANES:lo + 2 * LANES], cm, sam, sbm,
                                                       MLA_ROPE // 2).astype(BF16)
    ckv = _rms(ckv_ref[...], gkv_ref[...]).astype(BF16)
    kn = _dot(ckv, wuk_ref[...])
    kpe = _rope(kr_ref[...], cm, sam, sbm, MLA_ROPE // 2).astype(BF16)
    for h in range(MLA_HEADS):
        lo = h * MLA_QK_PAD
        km_ref[:, lo:lo + LANES] = kn[:, h * MLA_NOPE:(h + 1) * MLA_NOPE].astype(BF16)
        km_ref[:, lo + LANES:lo + 2 * LANES] = kpe
    vm_ref[...] = _dot(ckv, wuv_ref[...]).astype(BF16)

    for h in range(DSA_HEADS):
        sl = slice(h * LANES, (h + 1) * LANES)
        dqo_ref[:, sl] = _rope(dq_ref[:, sl], cd, sad, sbd, DSA_ROT // 2).astype(BF16)
    for h in range(DSA_KV_HEADS):
        sl = slice(h * LANES, (h + 1) * LANES)
        dko_ref[:, sl] = _rope(dk_ref[:, sl], cd, sad, sbd, DSA_ROT // 2).astype(BF16)
    dvo_ref[...] = dv_ref[...].astype(BF16)

    for c in range(IDX_HEADS * IDX_DIM // LANES):
        sl = slice(c * LANES, (c + 1) * LANES)
        iqo_ref[:, sl] = _rope(iq_ref[:, sl], ci, sai, sbi, IDX_ROT // 2).astype(BF16)
    lane = lax.broadcasted_iota(I32, ikw_ref.shape, 1)
    ka = jnp.where(lane < IDX_DIM, _rope(ikw_ref[...], ci, sai, sbi, IDX_ROT // 2), 0.0)
    ka_ref[...] = ka.astype(BF16)
    kb_ref[...] = pltpu.roll(ka, IDX_DIM, 1).astype(BF16)


def _prep(proj, tab_m, tab_d, tab_i, gq, gkv, wuq, wuk, wuv, tm=256):
    T = proj.shape[0]

    def col(width, off):
        return pl.BlockSpec((tm, width), lambda i: (i, off // width))

    def full(a):
        return pl.BlockSpec(a.shape, lambda i: (0,) * a.ndim)

    tab = pl.BlockSpec((3, tm, LANES), lambda i: (0, i, 0))
    widths = (MLA_HEADS * MLA_QK_PAD, MLA_HEADS * MLA_QK_PAD, MLA_HEADS * MLA_V, DSA_HEADS * DSA_HEAD_DIM,
              DSA_KV_HEADS * DSA_HEAD_DIM, DSA_KV_HEADS * DSA_HEAD_DIM, IDX_HEADS * IDX_DIM, LANES, LANES)
    return pl.pallas_call(
        _prep_kernel,
        grid=(T // tm,),
        in_specs=[col(MLA_Q_RANK, OFF_CQ), col(MLA_KV_RANK, OFF_CKV), col(LANES, OFF_KROPE),
                  col(1024, OFF_DQ), col(256, OFF_DK), col(256, OFF_DV), col(1024, OFF_IQ), col(LANES, OFF_IKW),
                  tab, tab, tab, full(gq), full(gkv), full(wuq), full(wuk), full(wuv)],
        out_specs=[pl.BlockSpec((tm, w), lambda i: (i, 0)) for w in widths],
        out_shape=[jax.ShapeDtypeStruct((T, w), BF16) for w in widths],
        compiler_params=_params("parallel"),
        name="prep",
    )(proj, proj, proj, proj, proj, proj, proj, proj, tab_m, tab_d, tab_i, gq, gkv, wuq, wuk, wuv)


def _mla_attn_tile(q_ref, k_ref, v_ref, o_ref, c, tq, scale):
    n = (c + 1) * tq
    q_chunk = (c * tq + lax.broadcasted_iota(I32, (tq, n), 0)) // CHUNK
    k_chunk = lax.broadcasted_iota(I32, (tq, n), 1) // CHUNK
    allowed = k_chunk <= q_chunk
    for h in range(MLA_HEADS):
        qk = slice(h * MLA_QK_PAD, (h + 1) * MLA_QK_PAD)
        s = _dot_nt(q_ref[:, qk], k_ref[0:n, qk]) * scale
        s = jnp.where(allowed, s, NEG_INF)
        p = jnp.exp(s - jnp.max(s, axis=-1, keepdims=True))
        l = jnp.sum(p, axis=-1, keepdims=True)
        vs = slice(h * MLA_V, (h + 1) * MLA_V)
        o_ref[:, vs] = (_dot(p.astype(BF16), v_ref[0:n, vs]) / l).astype(BF16)


def _mla_attn_kernel(q_ref, k_ref, v_ref, o_ref, *, tq, scale):
    for c in range(k_ref.shape[0] // tq):
        pl.when(pl.program_id(1) == c)(functools.partial(_mla_attn_tile, q_ref, k_ref, v_ref, o_ref, c, tq, scale))


def _mla_attn(q, k, v, B, S, tq=256):
    nq = S // tq
    return pl.pallas_call(
        functools.partial(_mla_attn_kernel, tq=tq, scale=(MLA_NOPE + MLA_ROPE) ** -0.5),
        grid=(B, nq),
        in_specs=[pl.BlockSpec((tq, q.shape[1]), lambda b, i: (b * nq + i, 0)),
                  pl.BlockSpec((S, k.shape[1]), lambda b, i: (b, 0)),
                  pl.BlockSpec((S, v.shape[1]), lambda b, i: (b, 0))],
        out_specs=pl.BlockSpec((tq, v.shape[1]), lambda b, i: (b * nq + i, 0)),
        out_shape=jax.ShapeDtypeStruct((B * S, v.shape[1]), BF16),
        compiler_params=_params("parallel", "arbitrary"),
        name="mla_attn",
    )(q, k, v)


def _order_key(score):
    bits = pltpu.bitcast(score + 0.0, I32)
    return bits ^ ((bits >> 31) & jnp.int32(0x7FFFFFFF))


def _dsa_body(iq_ref, ikw_ref, dq_ref, ka_ref, kb_ref, dk_ref, dv_ref, o_ref, S, top_k, scale):
    tq = Q_BLOCK
    q0 = pl.program_id(1) * tq
    q_chunk = (q0 + lax.broadcasted_iota(I32, (tq, S), 0)) // CHUNK
    k_chunk = lax.broadcasted_iota(I32, (tq, S), 1) // CHUNK
    allowed = k_chunk <= q_chunk

    w = ikw_ref[...] * (IDX_HEADS ** -0.5 * IDX_DIM ** -0.5)
    score = jnp.zeros((tq, S), F32)
    for c in range(IDX_HEADS // 2):
        qpair = iq_ref[:, c * LANES:(c + 1) * LANES]
        for half, k_ref in enumerate((ka_ref, kb_ref)):
            h = 2 * c + half
            score = score + w[:, IDX_DIM + h:IDX_DIM + h + 1] * jnp.maximum(_dot_nt(qpair, k_ref[0:S, :]), 0.0)
    score = jnp.where(allowed, score, NEG_INF)

    keys = _order_key(score)
    kf = jnp.float32(top_k)

    def count_ge(cand):
        return jnp.sum((keys >= cand).astype(F32), axis=-1, keepdims=True)

    int_min = jnp.full((tq, 1), jnp.iinfo(jnp.int32).min, I32)
    thr = jnp.where(count_ge(jnp.zeros((tq, 1), I32)) >= kf, 0, int_min)

    def step(i, thr):
        cand = thr + jnp.left_shift(jnp.int32(1), 30 - i)
        return jnp.where(count_ge(cand) >= kf, cand, thr)

    thr = lax.fori_loop(0, 31, step, thr)

    gt = keys > thr
    eq = keys == thr
    need = kf - jnp.sum(gt.astype(F32), axis=-1, keepdims=True)
    tri = (lax.broadcasted_iota(I32, (LANES, LANES), 0) <= lax.broadcasted_iota(I32, (LANES, LANES), 1)).astype(BF16)
    run = jnp.zeros((tq, 1), F32)
    sel = []
    for c in range(S // LANES):
        sl = slice(c * LANES, (c + 1) * LANES)
        prefix = _dot(eq[:, sl].astype(BF16), tri) + run
        sel.append(gt[:, sl] | (eq[:, sl] & (prefix <= need)))
        run = prefix[:, LANES - 1:LANES]
    mask = jnp.concatenate(sel, axis=1) & allowed

    mask_g = jnp.concatenate([mask] * DSA_GROUP, axis=0)
    for g in range(DSA_KV_HEADS):
        kv = slice(g * DSA_HEAD_DIM, (g + 1) * DSA_HEAD_DIM)
        qg = jnp.concatenate([dq_ref[:, (g * DSA_GROUP + n) * DSA_HEAD_DIM:(g * DSA_GROUP + n + 1) * DSA_HEAD_DIM]
                              for n in range(DSA_GROUP)], axis=0)
        s = _dot_nt(qg, dk_ref[0:S, kv]) * scale
        s = jnp.where(mask_g, s, NEG_INF)
        p = jnp.exp(s - jnp.max(s, axis=-1, keepdims=True))
        l = jnp.sum(p, axis=-1, keepdims=True)
        o = _dot(p.astype(BF16), dv_ref[0:S, kv]) / l
        for n in range(DSA_GROUP):
            hs = (g * DSA_GROUP + n) * DSA_HEAD_DIM
            o_ref[:, hs:hs + DSA_HEAD_DIM] = o[n * tq:(n + 1) * tq].astype(BF16)


DSA_EXTENTS = 4


def _dsa_kernel(*refs, top_k, scale):
    S = refs[3].shape[0]
    nq = S // Q_BLOCK
    nvar = DSA_EXTENTS if nq % DSA_EXTENTS == 0 else 1
    per = nq // nvar
    for v in range(nvar):
        pl.when(pl.program_id(1) // per == v)(
            functools.partial(_dsa_body, *refs, (v + 1) * per * Q_BLOCK, top_k, scale))


def _dsa(iq, proj, dq, ka, kb, dk, dv, B, S):
    tq = Q_BLOCK
    nq = S // tq

    def qblk(width, off=0):
        return pl.BlockSpec((tq, width), lambda b, i: (b * nq + i, off // width))

    def kblk(width):
        return pl.BlockSpec((S, width), lambda b, i: (b, 0))

    return pl.pallas_call(
        functools.partial(_dsa_kernel, top_k=min(DSA_TOPK_MAX, S // 4), scale=DSA_HEAD_DIM ** -0.5),
        grid=(B, nq),
        in_specs=[qblk(1024), qblk(LANES, OFF_IKW), qblk(1024), kblk(LANES), kblk(LANES), kblk(256), kblk(256)],
        out_specs=qblk(1024),
        out_shape=jax.ShapeDtypeStruct((B * S, DSA_HEADS * DSA_HEAD_DIM), BF16),
        compiler_params=_params("parallel", "arbitrary"),
        name="dsa",
    )(iq, proj, dq, ka, kb, dk, dv)


def _merge_kernel(oa_ref, ob_ref, ga_ref, gb_ref, wa_ref, wb_ref, o_ref):
    a = _dot(oa_ref[...], wa_ref[...])
    b = _dot(ob_ref[...], wb_ref[...])
    o_ref[...] = (jax.nn.sigmoid(ga_ref[...]) * a + jax.nn.sigmoid(gb_ref[...]) * b).astype(BF16)


def _merge(oa, ob, proj, wa, wb, tm=256):
    T = oa.shape[0]
    row = lambda w, c=0: pl.BlockSpec((tm, w), lambda i: (i, c))
    full = lambda a: pl.BlockSpec(a.shape, lambda i: (0, 0))
    return pl.pallas_call(
        _merge_kernel,
        grid=(T // tm,),
        in_specs=[row(1024), row(1024), row(D_MODEL, 0), row(D_MODEL, 1), full(wa), full(wb)],
        out_specs=row(D_MODEL),
        out_shape=jax.ShapeDtypeStruct((T, D_MODEL), BF16),
        compiler_params=_params("parallel"),
        name="merge",
    )(oa, ob, proj, proj, wa, wb)


def _outproj_kernel(x_ref, m_ref, wo_ref, g_ref, wq_ref, x1_ref, h2_ref, qp_ref):
    x1 = x_ref[...] + _dot(m_ref[...], wo_ref[...])
    x1_ref[...] = x1
    h2 = _rms(x1, g_ref[...]).astype(BF16)
    h2_ref[...] = h2
    qp_ref[...] = _dot(h2, wq_ref[...]).astype(BF16)


def _outproj(x, merged, wo, g, wq, tm=256):
    T = x.shape[0]
    row = lambda w: pl.BlockSpec((tm, w), lambda i: (i, 0))
    full = lambda a: pl.BlockSpec(a.shape, lambda i: (0, 0))
    nq = wq.shape[1]
    return pl.pallas_call(
        _outproj_kernel,
        grid=(T // tm,),
        in_specs=[row(D_MODEL), row(D_MODEL), full(wo), full(g), full(wq)],
        out_specs=[row(D_MODEL), row(D_MODEL), row(nq)],
        out_shape=[jax.ShapeDtypeStruct((T, D_MODEL), F32), jax.ShapeDtypeStruct((T, D_MODEL), BF16),
                   jax.ShapeDtypeStruct((T, nq), BF16)],
        compiler_params=_params("parallel"),
        name="outproj",
    )(x, merged, wo, g, wq)


def _top16(cur, pos=None, payload=None):
    if pos is None:
        pos = lax.broadcasted_iota(I32, cur.shape, 0).astype(F32)
    vals, idxs = [], []
    for _ in range(PEER_TOPK):
        m = jnp.max(cur, axis=0, keepdims=True)
        idx = jnp.min(jnp.where(cur == m, pos, jnp.inf), axis=0, keepdims=True)
        hit = pos == idx
        vals.append(m)
        if payload is None:
            idxs.append(idx)
        else:
            idxs.append(jnp.max(jnp.where(hit, payload, -1.0), axis=0, keepdims=True))
        cur = jnp.where(hit, -jnp.inf, cur)
    return jnp.concatenate(vals, axis=0), jnp.concatenate(idxs, axis=0)


def _pair_candidates(s1, i1, s2, i2):
    tb = s1.shape[1]
    sub = lax.broadcasted_iota(I32, (8, tb), 0).astype(F32)
    cs, cp, ci = ([s1[0:1] + s2], [lax.broadcasted_iota(I32, (PEER_TOPK, tb), 0).astype(F32)],
                  [i1[0:1] * PEER_N_KEYS + i2])
    for a in range(1, 8):
        keep = sub < PEER_TOPK // (a + 1)
        cs.append(jnp.where(keep, s1[a:a + 1] + s2[0:8], -jnp.inf))
        cp.append(a * PEER_TOPK + sub)
        ci.append(i1[a:a + 1] * PEER_N_KEYS + i2[0:8])
    cs.append(s1[8:16] + s2[0:1])
    cp.append((8 + sub) * PEER_TOPK)
    ci.append(i1[8:16] * PEER_N_KEYS + i2[0:1])
    return jnp.concatenate(cs, axis=0), jnp.concatenate(cp, axis=0), jnp.concatenate(ci, axis=0)


def _route_kernel(qp_ref, keys_ref, idx_ref, g_ref):
    for h in range(PEER_HEADS):
        qpair = qp_ref[:, h * PEER_KEY_DIM:(h + 1) * PEER_KEY_DIM]
        s1, i1 = _top16(_dot_nt(keys_ref[2 * h], qpair))
        s2, i2 = _top16(_dot_nt(keys_ref[2 * h + 1], qpair))
        cand_s, cand_p, cand_i = _pair_candidates(s1, i1, s2, i2)
        top_s, expert = _top16(cand_s, cand_p, cand_i)
        e = jnp.exp(top_s - top_s[0:1])
        g = e / jnp.sum(e, axis=0, keepdims=True)
        idx_ref[0, h * PEER_TOPK:(h + 1) * PEER_TOPK, :] = expert.astype(I32)
        g_ref[0, h * PEER_TOPK:(h + 1) * PEER_TOPK, :] = g


def _route(qp, keys_pad, tb=128):
    T = qp.shape[0]
    nb = T // tb
    nk = PEER_HEADS * PEER_TOPK
    out = pl.BlockSpec((1, nk, tb), lambda i: (i, 0, 0))
    return pl.pallas_call(
        _route_kernel,
        grid=(nb,),
        in_specs=[pl.BlockSpec((tb, qp.shape[1]), lambda i: (i, 0)),
                  pl.BlockSpec(keys_pad.shape, lambda i: (0, 0, 0))],
        out_specs=[out, out],
        out_shape=[jax.ShapeDtypeStruct((nb, nk, tb), I32), jax.ShapeDtypeStruct((nb, nk, tb), F32)],
        compiler_params=_params("parallel"),
        name="peer_route",
    )(qp, keys_pad)


PEER_TG = 8


PEER_EARLY = 4


def _peer_issue(idx_ref, t0, uv_hbm, buf, sem, first=0, last=PEER_TG):
    nk = PEER_HEADS * PEER_TOPK
    for t in range(first, last):
        for k in range(nk):
            pltpu.async_copy(uv_hbm.at[pl.ds(idx_ref[t0 + t, k], 1)], buf.at[pl.ds(t * nk + k, 1)], sem)


def _peer_wait(uv_hbm, buf, sem):
    pltpu.make_async_copy(uv_hbm.at[pl.ds(0, buf.shape[0])], buf, sem).wait()


def _peer_compute(buf, g, h, x1, gf):
    nk = PEER_HEADS * PEER_TOPK
    rows = PEER_TG * nk
    w = buf[...]
    ub = pltpu.bitcast(w << 16, F32).astype(BF16)
    vb = pltpu.bitcast(w & jnp.int32(-65536), F32).astype(BF16)
    res = _dot_nt(h, ub)
    row = lax.broadcasted_iota(I32, (PEER_TG, rows), 0)
    diag = (lax.broadcasted_iota(I32, (PEER_TG, rows), 1) // nk) == row
    res = jnp.where(diag, res, 0.0)
    a = res[:, 0:nk]
    for j in range(1, PEER_TG):
        a = a + res[:, j * nk:(j + 1) * nk]
    act = 0.5 * a * (1.0 + lax.erf(a * np.float32(2.0 ** -0.5)))
    coef = g * act
    coef_bd = jnp.where(diag, jnp.concatenate([coef] * PEER_TG, axis=1), 0.0).astype(BF16)
    out = _dot(coef_bd, vb)
    return _rms(x1 + out, gf)


def _peer_kernel(idx_ref, nxt_ref, g_ref, h_ref, x1_ref, gf_ref, uv_hbm, y_ref, buf_a, buf_b, sem):
    i = pl.program_id(0)
    lo, hi = slice(0, PEER_TG), slice(PEER_TG, 2 * PEER_TG)

    @pl.when(i == 0)
    def _():
        _peer_issue(idx_ref, 0, uv_hbm, buf_a, sem.at[0])

    _peer_issue(idx_ref, PEER_TG, uv_hbm, buf_b, sem.at[1], 0, PEER_EARLY)
    _peer_wait(uv_hbm, buf_a, sem.at[0])
    _peer_issue(idx_ref, PEER_TG, uv_hbm, buf_b, sem.at[1], PEER_EARLY, PEER_TG)
    y_ref[lo, :] = _peer_compute(buf_a, g_ref[lo, :], h_ref[lo, :], x1_ref[lo, :], gf_ref[...])
    _peer_issue(nxt_ref, 0, uv_hbm, buf_a, sem.at[0], 0, PEER_EARLY)
    _peer_wait(uv_hbm, buf_b, sem.at[1])
    _peer_issue(nxt_ref, 0, uv_hbm, buf_a, sem.at[0], PEER_EARLY, PEER_TG)
    y_ref[hi, :] = _peer_compute(buf_b, g_ref[hi, :], h_ref[hi, :], x1_ref[hi, :], gf_ref[...])

    @pl.when(i == pl.num_programs(0) - 1)
    def _():
        _peer_wait(uv_hbm, buf_a, sem.at[0])


def _peer(idx, g, h2, x1, gf, uv):
    T = idx.shape[0]
    nk = idx.shape[1]
    ts = 2 * PEER_TG
    n = T // ts
    row = lambda w: pl.BlockSpec((ts, w), lambda i: (i, 0))
    return pl.pallas_call(
        _peer_kernel,
        grid=(n,),
        in_specs=[pl.BlockSpec((ts, nk), lambda i: (i, 0), memory_space=pltpu.SMEM),
                  pl.BlockSpec((ts, nk), lambda i: (jnp.minimum(i + 1, n - 1), 0), memory_space=pltpu.SMEM),
                  row(nk), row(D_MODEL), row(D_MODEL), pl.BlockSpec(gf.shape, lambda i: (0, 0)),
                  pl.BlockSpec(memory_space=pl.ANY)],
        out_specs=row(D_MODEL),
        out_shape=jax.ShapeDtypeStruct((T, D_MODEL), F32),
        scratch_shapes=[pltpu.VMEM((PEER_TG * nk, D_MODEL), I32), pltpu.VMEM((PEER_TG * nk, D_MODEL), I32),
                        pltpu.SemaphoreType.DMA((2,))],
        compiler_params=_params("arbitrary"),
        name="peer_ffn",
    )(idx, idx, g, h2, x1, gf, uv)


def _rope_tables(pos, rot, width):
    half = rot // 2
    inv_freq = ROPE_THETA ** (-(jnp.arange(half, dtype=F32) * 2.0) / rot)
    ang = pos.astype(F32)[:, None] * inv_freq
    cos, sin = jnp.cos(ang), jnp.sin(ang)
    T = pos.shape[0]
    one = jnp.ones((T, width - rot), F32)
    zero = jnp.zeros((T, width - rot), F32)
    zh = jnp.zeros((T, half), F32)
    c = jnp.concatenate([cos, cos, one], axis=1)
    sa = jnp.concatenate([zh, sin, zero], axis=1)
    sb = jnp.concatenate([-sin, zh, zero], axis=1)
    reps = LANES // width
    return jnp.stack([jnp.tile(t, (1, reps)) for t in (c, sa, sb)])


def kernel(x, positions, norm_mix_g, w_in, mla_q_norm_g, mla_kv_norm_g, mla_w_uq, mla_w_uk, mla_w_uv, w_branch_a, w_branch_b, w_out, norm_ffn_g, peer_w_q, peer_sub_keys, peer_u, peer_v, norm_final_g):
    B, S, D = x.shape
    T = B * S
    assert D == D_MODEL and w_in.shape[0] == 1 and S % 256 == 0 and T % 512 == 0
    xf = x.reshape(T, D)
    pos = positions.reshape(T)

    splits = [int(c) for c in np.cumsum(IN_SIZES)[:-1]]
    w_cq, w_ckv, w_kr, w_dq, w_dk, w_dv, w_iq, w_ik, w_iw, w_gates = jnp.split(w_in[0], splits, axis=1)
    zc = lambda n: jnp.zeros((D, n), F32)
    w_pack = jnp.concatenate([w_gates, w_dq, w_iq, w_cq, w_ckv, w_dk, w_dv, w_kr, zc(64), w_ik, w_iw, zc(48)],
                             axis=1).astype(BF16)
    wuq = jnp.pad(mla_w_uq[0], ((0, 0), (0, 0), (0, MLA_QK_PAD - MLA_NOPE - MLA_ROPE)))
    wuq = wuq.reshape(MLA_Q_RANK, MLA_HEADS * MLA_QK_PAD).astype(BF16)
    wuk = mla_w_uk[0].reshape(MLA_KV_RANK, MLA_HEADS * MLA_NOPE).astype(BF16)
    wuv = mla_w_uv[0].reshape(MLA_KV_RANK, MLA_HEADS * MLA_V).astype(BF16)
    wq_peer = peer_w_q[0].reshape(D, PEER_HEADS * PEER_KEY_DIM).astype(BF16)
    sk = peer_sub_keys[0].reshape(PEER_HEADS * 2, PEER_N_KEYS, PEER_KEY_DIM // 2)
    zk = jnp.zeros_like(sk)
    first = (jnp.arange(PEER_HEADS * 2) % 2 == 0)[:, None, None]
    keys_pad = jnp.where(first, jnp.concatenate([sk, zk], axis=2), jnp.concatenate([zk, sk], axis=2)).astype(BF16)

    tab_m = _rope_tables(pos, MLA_ROPE, LANES)
    tab_d = _rope_tables(pos, DSA_ROT, DSA_HEAD_DIM)
    tab_i = _rope_tables(pos, IDX_ROT, IDX_DIM)

    proj = _inproj(xf, norm_mix_g[0][None], w_pack)
    qm, km, vm, dq, dk, dv, iq, ka, kb = _prep(proj, tab_m, tab_d, tab_i, mla_q_norm_g[0][None],
                                               mla_kv_norm_g[0][None], wuq, wuk, wuv)
    o_a = _mla_attn(qm, km, vm, B, S)
    o_b = _dsa(iq, proj, dq, ka, kb, dk, dv, B, S)
    merged = _merge(o_a, o_b, proj, w_branch_a[0].astype(BF16), w_branch_b[0].astype(BF16))
    x1, h2, qp = _outproj(xf, merged, w_out[0].astype(BF16), norm_ffn_g[0][None], wq_peer)
    idx_t, g_t = _route(qp, keys_pad)
    nk = PEER_HEADS * PEER_TOPK
    idx = idx_t.transpose(0, 2, 1).reshape(T, nk)
    g = g_t.transpose(0, 2, 1).reshape(T, nk)
    half = lambda t: lax.bitcast_convert_type(t.astype(BF16), jnp.uint16).astype(jnp.uint32)
    uv = lax.bitcast_convert_type(half(peer_u[0]) | (half(peer_v[0]) << 16), I32)
    y = _peer(idx, g, h2, x1, norm_final_g[None], uv)
    return y.reshape(B, S, D)
```

```python
import functools

import numpy as np
import jax
import jax.numpy as jnp
from jax import lax
from jax.experimental import pallas as pl
from jax.experimental.pallas import tpu as pltpu

F32 = jnp.float32
BF16 = jnp.bfloat16
I32 = jnp.int32

D_MODEL = 2048
CHUNK = 64
Q_BLOCK = 128
ROPE_THETA = 500000.0
EPS = 1e-6
NEG_INF = -1e30

MLA_HEADS = 8
MLA_NOPE = 128
MLA_ROPE = 64
MLA_V = 128
MLA_Q_RANK = 512
MLA_KV_RANK = 256
MLA_QK_PAD = 256

DSA_HEADS = 8
DSA_KV_HEADS = 2
DSA_GROUP = DSA_HEADS // DSA_KV_HEADS
DSA_HEAD_DIM = 128
DSA_ROT = DSA_HEAD_DIM // 4
IDX_HEADS = 16
IDX_DIM = 64
IDX_ROT = IDX_DIM // 4
DSA_TOPK_MAX = 256

PEER_HEADS = 8
PEER_N_KEYS = 128
PEER_KEY_DIM = 128
PEER_TOPK = 16

IN_SIZES = (MLA_Q_RANK, MLA_KV_RANK, MLA_ROPE, DSA_HEADS * DSA_HEAD_DIM, DSA_KV_HEADS * DSA_HEAD_DIM,
            DSA_KV_HEADS * DSA_HEAD_DIM, IDX_HEADS * IDX_DIM, IDX_DIM, IDX_HEADS, 2 * D_MODEL)

LANES = 128

OFF_GATES = 0
OFF_DQ = 4096
OFF_IQ = 5120
OFF_CQ = 6144
OFF_CKV = 6656
OFF_DK = 6912
OFF_DV = 7168
OFF_KROPE = 7424
OFF_IKW = 7552
N_PACK = 7680

VMEM_LIMIT = 56 * 1024 * 1024


def _params(*sem):
    return pltpu.CompilerParams(dimension_semantics=sem, vmem_limit_bytes=VMEM_LIMIT)


def _rms(t, g):
    return t * lax.rsqrt(jnp.mean(t * t, axis=-1, keepdims=True) + EPS) * g


def _dot(a, b):
    return jnp.dot(a, b, preferred_element_type=F32)


def _dot_nt(a, b):
    return lax.dot_general(a, b, (((1,), (1,)), ((), ())), preferred_element_type=F32)


def _rope(t, c, sa, sb, half):
    return t * c + pltpu.roll(t, half, 1) * sa + pltpu.roll(t, LANES - half, 1) * sb


def _inproj_kernel(x_ref, g_ref, w_ref, o_ref, h_ref):
    @pl.when(pl.program_id(1) == 0)
    def _():
        h_ref[...] = _rms(x_ref[...], g_ref[...]).astype(BF16)

    o_ref[...] = _dot(h_ref[...], w_ref[...])


def _inproj(x, g, w, tm=512, tn=1920):
    T = x.shape[0]
    return pl.pallas_call(
        _inproj_kernel,
        grid=(T // tm, N_PACK // tn),
        in_specs=[pl.BlockSpec((tm, D_MODEL), lambda i, j: (i, 0)),
                  pl.BlockSpec((1, D_MODEL), lambda i, j: (0, 0)),
                  pl.BlockSpec((D_MODEL, tn), lambda i, j: (0, j))],
        out_specs=pl.BlockSpec((tm, tn), lambda i, j: (i, j)),
        out_shape=jax.ShapeDtypeStruct((T, N_PACK), F32),
        scratch_shapes=[pltpu.VMEM((tm, D_MODEL), BF16)],
        compiler_params=_params("parallel", "arbitrary"),
        name="inproj",
    )(x, g, w)


def _prep_kernel(cq_ref, ckv_ref, kr_ref, dq_ref, dk_ref, dv_ref, iq_ref, ikw_ref,
                 tm_ref, td_ref, ti_ref, gq_ref, gkv_ref, wuq_ref, wuk_ref, wuv_ref,
                 qm_ref, km_ref, vm_ref, dqo_ref, dko_ref, dvo_ref, iqo_ref, ka_ref, kb_ref):
    cm, sam, sbm = tm_ref[0], tm_ref[1], tm_ref[2]
    cd, sad, sbd = td_ref[0], td_ref[1], td_ref[2]
    ci, sai, sbi = ti_ref[0], ti_ref[1], ti_ref[2]

    qn = _rms(cq_ref[...], gq_ref[...]).astype(BF16)
    q = _dot(qn, wuq_ref[...])
    for h in range(MLA_HEADS):
        lo = h * MLA_QK_PAD
        qm_ref[:, lo:lo + LANES] = q[:, lo:lo + LANES].astype(BF16)
        qm_ref[:, lo + LANES:lo + 2 * LANES] = _rope(q[:, lo + LANES:lo + 2 * LANES], cm, sam, sbm,
                                                       MLA_ROPE // 2).astype(BF16)
    ckv = _rms(ckv_ref[...], gkv_ref[...]).astype(BF16)
    kn = _dot(ckv, wuk_ref[...])
    kpe = _rope(kr_ref[...], cm, sam, sbm, MLA_ROPE // 2).astype(BF16)
    for h in range(MLA_HEADS):
        lo = h * MLA_QK_PAD
        km_ref[:, lo:lo + LANES] = kn[:, h * MLA_NOPE:(h + 1) * MLA_NOPE].astype(BF16)
        km_ref[:, lo + LANES:lo + 2 * LANES] = kpe
    vm_ref[...] = _dot(ckv, wuv_ref[...]).astype(BF16)

    for h in range(DSA_HEADS):
        sl = slice(h * LANES, (h + 1) * LANES)
        dqo_ref[:, sl] = _rope(dq_ref[:, sl], cd, sad, sbd, DSA_ROT // 2).astype(BF16)
    for h in range(DSA_KV_HEADS):
        sl = slice(h * LANES, (h + 1) * LANES)
        dko_ref[:, sl] = _rope(dk_ref[:, sl], cd, sad, sbd, DSA_ROT // 2).astype(BF16)
    dvo_ref[...] = dv_ref[...].astype(BF16)

    for c in range(IDX_HEADS * IDX_DIM // LANES):
        sl = slice(c * LANES, (c + 1) * LANES)
        iqo_ref[:, sl] = _rope(iq_ref[:, sl], ci, sai, sbi, IDX_ROT // 2).astype(BF16)
    lane = lax.broadcasted_iota(I32, ikw_ref.shape, 1)
    ka = jnp.where(lane < IDX_DIM, _rope(ikw_ref[...], ci, sai, sbi, IDX_ROT // 2), 0.0)
    ka_ref[...] = ka.astype(BF16)
    kb_ref[...] = pltpu.roll(ka, IDX_DIM, 1).astype(BF16)


def _prep(proj, tab_m, tab_d, tab_i, gq, gkv, wuq, wuk, wuv, tm=256):
    T = proj.shape[0]

    def col(width, off):
        return pl.BlockSpec((tm, width), lambda i: (i, off // width))

    def full(a):
        return pl.BlockSpec(a.shape, lambda i: (0,) * a.ndim)

    tab = pl.BlockSpec((3, tm, LANES), lambda i: (0, i, 0))
    widths = (MLA_HEADS * MLA_QK_PAD, MLA_HEADS * MLA_QK_PAD, MLA_HEADS * MLA_V, DSA_HEADS * DSA_HEAD_DIM,
              DSA_KV_HEADS * DSA_HEAD_DIM, DSA_KV_HEADS * DSA_HEAD_DIM, IDX_HEADS * IDX_DIM, LANES, LANES)
    return pl.pallas_call(
        _prep_kernel,
        grid=(T // tm,),
        in_specs=[col(MLA_Q_RANK, OFF_CQ), col(MLA_KV_RANK, OFF_CKV), col(LANES, OFF_KROPE),
                  col(1024, OFF_DQ), col(256, OFF_DK), col(256, OFF_DV), col(1024, OFF_IQ), col(LANES, OFF_IKW),
                  tab, tab, tab, full(gq), full(gkv), full(wuq), full(wuk), full(wuv)],
        out_specs=[pl.BlockSpec((tm, w), lambda i: (i, 0)) for w in widths],
        out_shape=[jax.ShapeDtypeStruct((T, w), BF16) for w in widths],
        compiler_params=_params("parallel"),
        name="prep",
    )(proj, proj, proj, proj, proj, proj, proj, proj, tab_m, tab_d, tab_i, gq, gkv, wuq, wuk, wuv)


def _mla_attn_tile(q_ref, k_ref, v_ref, o_ref, c, tq, scale):
    n = (c + 1) * tq
    q_chunk = (c * tq + lax.broadcasted_iota(I32, (tq, n), 0)) // CHUNK
    k_chunk = lax.broadcasted_iota(I32, (tq, n), 1) // CHUNK
    allowed = k_chunk <= q_chunk
    for h in range(MLA_HEADS):
        qk = slice(h * MLA_QK_PAD, (h + 1) * MLA_QK_PAD)
        s = _dot_nt(q_ref[:, qk], k_ref[0:n, qk]) * scale
        s = jnp.where(allowed, s, NEG_INF)
        p = jnp.exp(s - jnp.max(s, axis=-1, keepdims=True))
        l = jnp.sum(p, axis=-1, keepdims=True)
        vs = slice(h * MLA_V, (h + 1) * MLA_V)
        o_ref[:, vs] = (_dot(p.astype(BF16), v_ref[0:n, vs]) / l).astype(BF16)


def _mla_attn_kernel(q_ref, k_ref, v_ref, o_ref, *, tq, scale):
    for c in range(k_ref.shape[0] // tq):
        pl.when(pl.program_id(1) == c)(functools.partial(_mla_attn_tile, q_ref, k_ref, v_ref, o_ref, c, tq, scale))


def _mla_attn(q, k, v, B, S, tq=256):
    nq = S // tq
    return pl.pallas_call(
        functools.partial(_mla_attn_kernel, tq=tq, scale=(MLA_NOPE + MLA_ROPE) ** -0.5),
        grid=(B, nq),
        in_specs=[pl.BlockSpec((tq, q.shape[1]), lambda b, i: (b * nq + i, 0)),
                  pl.BlockSpec((S, k.shape[1]), lambda b, i: (b, 0)),
                  pl.BlockSpec((S, v.shape[1]), lambda b, i: (b, 0))],
        out_specs=pl.BlockSpec((tq, v.shape[1]), lambda b, i: (b * nq + i, 0)),
        out_shape=jax.ShapeDtypeStruct((B * S, v.shape[1]), BF16),
        compiler_params=_params("parallel", "arbitrary"),
        name="mla_attn",
    )(q, k, v)


def _order_key(score):
    bits = pltpu.bitcast(score + 0.0, I32)
    return bits ^ ((bits >> 31) & jnp.int32(0x7FFFFFFF))


def _dsa_body(iq_ref, ikw_ref, dq_ref, ka_ref, kb_ref, dk_ref, dv_ref, o_ref, S, top_k, scale):
    tq = Q_BLOCK
    q0 = pl.program_id(1) * tq
    q_chunk = (q0 + lax.broadcasted_iota(I32, (tq, S), 0)) // CHUNK
    k_chunk = lax.broadcasted_iota(I32, (tq, S), 1) // CHUNK
    allowed = k_chunk <= q_chunk

    w = ikw_ref[...] * (IDX_HEADS ** -0.5 * IDX_DIM ** -0.5)
    score = jnp.zeros((tq, S), F32)
    for c in range(IDX_HEADS // 2):
        qpair = iq_ref[:, c * LANES:(c + 1) * LANES]
        for half, k_ref in enumerate((ka_ref, kb_ref)):
            h = 2 * c + half
            score = score + w[:, IDX_DIM + h:IDX_DIM + h + 1] * jnp.maximum(_dot_nt(qpair, k_ref[0:S, :]), 0.0)
    score = jnp.where(allowed, score, NEG_INF)

    keys = _order_key(score)
    kf = jnp.float32(top_k)

    def count_ge(cand):
        return jnp.sum((keys >= cand).astype(F32), axis=-1, keepdims=True)

    int_min = jnp.full((tq, 1), jnp.iinfo(jnp.int32).min, I32)
    thr = jnp.where(count_ge(jnp.zeros((tq, 1), I32)) >= kf, 0, int_min)

    def step(i, thr):
        cand = thr + jnp.left_shift(jnp.int32(1), 30 - i)
        return jnp.where(count_ge(cand) >= kf, cand, thr)

    thr = lax.fori_loop(0, 31, step, thr)

    gt = keys > thr
    eq = keys == thr
    need = kf - jnp.sum(gt.astype(F32), axis=-1, keepdims=True)
    tri = (lax.broadcasted_iota(I32, (LANES, LANES), 0) <= lax.broadcasted_iota(I32, (LANES, LANES), 1)).astype(BF16)
    run = jnp.zeros((tq, 1), F32)
    sel = []
    for c in range(S // LANES):
        sl = slice(c * LANES, (c + 1) * LANES)
        prefix = _dot(eq[:, sl].astype(BF16), tri) + run
        sel.append(gt[:, sl] | (eq[:, sl] & (prefix <= need)))
        run = prefix[:, LANES - 1:LANES]
    mask = jnp.concatenate(sel, axis=1) & allowed

    mask_g = jnp.concatenate([mask] * DSA_GROUP, axis=0)
    for g in range(DSA_KV_HEADS):
        kv = slice(g * DSA_HEAD_DIM, (g + 1) * DSA_HEAD_DIM)
        qg = jnp.concatenate([dq_ref[:, (g * DSA_GROUP + n) * DSA_HEAD_DIM:(g * DSA_GROUP + n + 1) * DSA_HEAD_DIM]
                              for n in range(DSA_GROUP)], axis=0)
        s = _dot_nt(qg, dk_ref[0:S, kv]) * scale
        s = jnp.where(mask_g, s, NEG_INF)
        p = jnp.exp(s - jnp.max(s, axis=-1, keepdims=True))
        l = jnp.sum(p, axis=-1, keepdims=True)
        o = _dot(p.astype(BF16), dv_ref[0:S, kv]) / l
        for n in range(DSA_GROUP):
            hs = (g * DSA_GROUP + n) * DSA_HEAD_DIM
            o_ref[:, hs:hs + DSA_HEAD_DIM] = o[n * tq:(n + 1) * tq].astype(BF16)


DSA_EXTENTS = 8


def _dsa_kernel(*refs, top_k, scale):
    S = refs[3].shape[0]
    nq = S // Q_BLOCK
    nvar = DSA_EXTENTS if nq % DSA_EXTENTS == 0 and S // DSA_EXTENTS >= top_k else 1
    per = nq // nvar
    for v in range(nvar):
        pl.when(pl.program_id(1) // per == v)(
            functools.partial(_dsa_body, *refs, (v + 1) * per * Q_BLOCK, top_k, scale))


def _dsa(iq, proj, dq, ka, kb, dk, dv, B, S):
    tq = Q_BLOCK
    nq = S // tq

    def qblk(width, off=0):
        return pl.BlockSpec((tq, width), lambda b, i: (b * nq + i, off // width))

    def kblk(width):
        return pl.BlockSpec((S, width), lambda b, i: (b, 0))

    return pl.pallas_call(
        functools.partial(_dsa_kernel, top_k=min(DSA_TOPK_MAX, S // 4), scale=DSA_HEAD_DIM ** -0.5),
        grid=(B, nq),
        in_specs=[qblk(1024), qblk(LANES, OFF_IKW), qblk(1024), kblk(LANES), kblk(LANES), kblk(256), kblk(256)],
        out_specs=qblk(1024),
        out_shape=jax.ShapeDtypeStruct((B * S, DSA_HEADS * DSA_HEAD_DIM), BF16),
        compiler_params=_params("parallel", "arbitrary"),
        name="dsa",
    )(iq, proj, dq, ka, kb, dk, dv)


def _merge_kernel(oa_ref, ob_ref, ga_ref, gb_ref, wa_ref, wb_ref, o_ref):
    a = _dot(oa_ref[...], wa_ref[...])
    b = _dot(ob_ref[...], wb_ref[...])
    o_ref[...] = (jax.nn.sigmoid(ga_ref[...]) * a + jax.nn.sigmoid(gb_ref[...]) * b).astype(BF16)


def _merge(oa, ob, proj, wa, wb, tm=256):
    T = oa.shape[0]
    row = lambda w, c=0: pl.BlockSpec((tm, w), lambda i: (i, c))
    full = lambda a: pl.BlockSpec(a.shape, lambda i: (0, 0))
    return pl.pallas_call(
        _merge_kernel,
        grid=(T // tm,),
        in_specs=[row(1024), row(1024), row(D_MODEL, 0), row(D_MODEL, 1), full(wa), full(wb)],
        out_specs=row(D_MODEL),
        out_shape=jax.ShapeDtypeStruct((T, D_MODEL), BF16),
        compiler_params=_params("parallel"),
        name="merge",
    )(oa, ob, proj, proj, wa, wb)


def _outproj_kernel(x_ref, m_ref, wo_ref, g_ref, wq_ref, x1_ref, h2_ref, qp_ref):
    x1 = x_ref[...] + _dot(m_ref[...], wo_ref[...])
    x1_ref[...] = x1
    h2 = _rms(x1, g_ref[...]).astype(BF16)
    h2_ref[...] = h2
    qp_ref[...] = _dot(h2, wq_ref[...]).astype(BF16)


def _outproj(x, merged, wo, g, wq, tm=256):
    T = x.shape[0]
    row = lambda w: pl.BlockSpec((tm, w), lambda i: (i, 0))
    full = lambda a: pl.BlockSpec(a.shape, lambda i: (0, 0))
    nq = wq.shape[1]
    return pl.pallas_call(
        _outproj_kernel,
        grid=(T // tm,),
        in_specs=[row(D_MODEL), row(D_MODEL), full(wo), full(g), full(wq)],
        out_specs=[row(D_MODEL), row(D_MODEL), row(nq)],
        out_shape=[jax.ShapeDtypeStruct((T, D_MODEL), F32), jax.ShapeDtypeStruct((T, D_MODEL), BF16),
                   jax.ShapeDtypeStruct((T, nq), BF16)],
        compiler_params=_params("parallel"),
        name="outproj",
    )(x, merged, wo, g, wq)


def _top16(cur, pos=None, payload=None):
    if pos is None:
        pos = lax.broadcasted_iota(I32, cur.shape, 0).astype(F32)
    vals, idxs = [], []
    for _ in range(PEER_TOPK):
        m = jnp.max(cur, axis=0, keepdims=True)
        idx = jnp.min(jnp.where(cur == m, pos, jnp.inf), axis=0, keepdims=True)
        hit = pos == idx
        vals.append(m)
        if payload is None:
            idxs.append(idx)
        else:
            idxs.append(jnp.max(jnp.where(hit, payload, -1.0), axis=0, keepdims=True))
        cur = jnp.where(hit, -jnp.inf, cur)
    return jnp.concatenate(vals, axis=0), jnp.concatenate(idxs, axis=0)


def _pair_candidates(s1, i1, s2, i2):
    tb = s1.shape[1]
    sub = lax.broadcasted_iota(I32, (8, tb), 0).astype(F32)
    cs, cp, ci = ([s1[0:1] + s2], [lax.broadcasted_iota(I32, (PEER_TOPK, tb), 0).astype(F32)],
                  [i1[0:1] * PEER_N_KEYS + i2])
    for a in range(1, 8):
        keep = sub < PEER_TOPK // (a + 1)
        cs.append(jnp.where(keep, s1[a:a + 1] + s2[0:8], -jnp.inf))
        cp.append(a * PEER_TOPK + sub)
        ci.append(i1[a:a + 1] * PEER_N_KEYS + i2[0:8])
    cs.append(s1[8:16] + s2[0:1])
    cp.append((8 + sub) * PEER_TOPK)
    ci.append(i1[8:16] * PEER_N_KEYS + i2[0:1])
    return jnp.concatenate(cs, axis=0), jnp.concatenate(cp, axis=0), jnp.concatenate(ci, axis=0)


def _route_kernel(qp_ref, keys_ref, idx_ref, g_ref):
    for h in range(PEER_HEADS):
        qpair = qp_ref[:, h * PEER_KEY_DIM:(h + 1) * PEER_KEY_DIM]
        s1, i1 = _top16(_dot_nt(keys_ref[2 * h], qpair))
        s2, i2 = _top16(_dot_nt(keys_ref[2 * h + 1], qpair))
        cand_s, cand_p, cand_i = _pair_candidates(s1, i1, s2, i2)
        top_s, expert = _top16(cand_s, cand_p, cand_i)
        e = jnp.exp(top_s - top_s[0:1])
        g = e / jnp.sum(e, axis=0, keepdims=True)
        idx_ref[0, h * PEER_TOPK:(h + 1) * PEER_TOPK, :] = expert.astype(I32)
        g_ref[0, h * PEER_TOPK:(h + 1) * PEER_TOPK, :] = g


def _route(qp, keys_pad, tb=128):
    T = qp.shape[0]
    nb = T // tb
    nk = PEER_HEADS * PEER_TOPK
    out = pl.BlockSpec((1, nk, tb), lambda i: (i, 0, 0))
    return pl.pallas_call(
        _route_kernel,
        grid=(nb,),
        in_specs=[pl.BlockSpec((tb, qp.shape[1]), lambda i: (i, 0)),
                  pl.BlockSpec(keys_pad.shape, lambda i: (0, 0, 0))],
        out_specs=[out, out],
        out_shape=[jax.ShapeDtypeStruct((nb, nk, tb), I32), jax.ShapeDtypeStruct((nb, nk, tb), F32)],
        compiler_params=_params("parallel"),
        name="peer_route",
    )(qp, keys_pad)


PEER_TG = 8
PEER_EARLY = 4


def _peer_issue(idx_ref, t0, uv_hbm, buf, sem, first=0, last=PEER_TG):
    nk = PEER_HEADS * PEER_TOPK
    for t in range(first, last):
        for k in range(nk):
            pltpu.async_copy(uv_hbm.at[pl.ds(idx_ref[t0 + t, k], 1)], buf.at[pl.ds(t * nk + k, 1)], sem)


def _peer_wait(uv_hbm, buf, sem):
    pltpu.make_async_copy(uv_hbm.at[pl.ds(0, buf.shape[0])], buf, sem).wait()


def _peer_compute(buf, g, h, x1, gf):
    nk = PEER_HEADS * PEER_TOPK
    rows = PEER_TG * nk
    w = buf[...]
    ub = pltpu.bitcast(w << 16, F32).astype(BF16)
    vb = pltpu.bitcast(w & jnp.int32(-65536), F32).astype(BF16)
    res = _dot_nt(h, ub)
    row = lax.broadcasted_iota(I32, (PEER_TG, rows), 0)
    diag = (lax.broadcasted_iota(I32, (PEER_TG, rows), 1) // nk) == row
    res = jnp.where(diag, res, 0.0)
    a = res[:, 0:nk]
    for j in range(1, PEER_TG):
        a = a + res[:, j * nk:(j + 1) * nk]
    act = 0.5 * a * (1.0 + lax.erf(a * np.float32(2.0 ** -0.5)))
    coef = g * act
    coef_bd = jnp.where(diag, jnp.concatenate([coef] * PEER_TG, axis=1), 0.0).astype(BF16)
    out = _dot(coef_bd, vb)
    return _rms(x1 + out, gf)


def _peer_kernel(idx_ref, nxt_ref, g_ref, h_ref, x1_ref, gf_ref, uv_hbm, y_ref, buf_a, buf_b, sem):
    i = pl.program_id(0)
    lo, hi = slice(0, PEER_TG), slice(PEER_TG, 2 * PEER_TG)

    @pl.when(i == 0)
    def _():
        _peer_issue(idx_ref, 0, uv_hbm, buf_a, sem.at[0])

    _peer_issue(idx_ref, PEER_TG, uv_hbm, buf_b, sem.at[1], 0, PEER_EARLY)
    _peer_wait(uv_hbm, buf_a, sem.at[0])
    _peer_issue(idx_ref, PEER_TG, uv_hbm, buf_b, sem.at[1], PEER_EARLY, PEER_TG)
    y_ref[lo, :] = _peer_compute(buf_a, g_ref[lo, :], h_ref[lo, :], x1_ref[lo, :], gf_ref[...])
    _peer_issue(nxt_ref, 0, uv_hbm, buf_a, sem.at[0], 0, PEER_EARLY)
    _peer_wait(uv_hbm, buf_b, sem.at[1])
    _peer_issue(nxt_ref, 0, uv_hbm, buf_a, sem.at[0], PEER_EARLY, PEER_TG)
    y_ref[hi, :] = _peer_compute(buf_b, g_ref[hi, :], h_ref[hi, :], x1_ref[hi, :], gf_ref[...])

    @pl.when(i == pl.num_programs(0) - 1)
    def _():
        _peer_wait(uv_hbm, buf_a, sem.at[0])


def _peer(idx, g, h2, x1, gf, uv):
    T = idx.shape[0]
    nk = idx.shape[1]
    ts = 2 * PEER_TG
    n = T // ts
    row = lambda w: pl.BlockSpec((ts, w), lambda i: (i, 0))
    return pl.pallas_call(
        _peer_kernel,
        grid=(n,),
        in_specs=[pl.BlockSpec((ts, nk), lambda i: (i, 0), memory_space=pltpu.SMEM),
                  pl.BlockSpec((ts, nk), lambda i: (jnp.minimum(i + 1, n - 1), 0), memory_space=pltpu.SMEM),
                  row(nk), row(D_MODEL), row(D_MODEL), pl.BlockSpec(gf.shape, lambda i: (0, 0)),
                  pl.BlockSpec(memory_space=pl.ANY)],
        out_specs=row(D_MODEL),
        out_shape=jax.ShapeDtypeStruct((T, D_MODEL), F32),
        scratch_shapes=[pltpu.VMEM((PEER_TG * nk, D_MODEL), I32), pltpu.VMEM((PEER_TG * nk, D_MODEL), I32),
                        pltpu.SemaphoreType.DMA((2,))],
        compiler_params=_params("arbitrary"),
        name="peer_ffn",
    )(idx, idx, g, h2, x1, gf, uv)


def _rope_tables(pos, rot, width):
    half = rot // 2
    inv_freq = ROPE_THETA ** (-(jnp.arange(half, dtype=F32) * 2.0) / rot)
    ang = pos.astype(F32)[:, None] * inv_freq
    cos, sin = jnp.cos(ang), jnp.sin(ang)
    T = pos.shape[0]
    one = jnp.ones((T, width - rot), F32)
    zero = jnp.zeros((T, width - rot), F32)
    zh = jnp.zeros((T, half), F32)
    c = jnp.concatenate([cos, cos, one], axis=1)
    sa = jnp.concatenate([zh, sin, zero], axis=1)
    sb = jnp.concatenate([-sin, zh, zero], axis=1)
    reps = LANES // width
    return jnp.stack([jnp.tile(t, (1, reps)) for t in (c, sa, sb)])


def kernel(x, positions, norm_mix_g, w_in, mla_q_norm_g, mla_kv_norm_g, mla_w_uq, mla_w_uk, mla_w_uv, w_branch_a, w_branch_b, w_out, norm_ffn_g, peer_w_q, peer_sub_keys, peer_u, peer_v, norm_final_g):
    B, S, D = x.shape
    T = B * S
    assert D == D_MODEL and w_in.shape[0] == 1 and S % 256 == 0 and T % 512 == 0
    xf = x.reshape(T, D)
    pos = positions.reshape(T)

    splits = [int(c) for c in np.cumsum(IN_SIZES)[:-1]]
    w_cq, w_ckv, w_kr, w_dq, w_dk, w_dv, w_iq, w_ik, w_iw, w_gates = jnp.split(w_in[0], splits, axis=1)
    zc = lambda n: jnp.zeros((D, n), F32)
    w_pack = jnp.concatenate([w_gates, w_dq, w_iq, w_cq, w_ckv, w_dk, w_dv, w_kr, zc(64), w_ik, w_iw, zc(48)],
                             axis=1).astype(BF16)
    wuq = jnp.pad(mla_w_uq[0], ((0, 0), (0, 0), (0, MLA_QK_PAD - MLA_NOPE - MLA_ROPE)))
    wuq = wuq.reshape(MLA_Q_RANK, MLA_HEADS * MLA_QK_PAD).astype(BF16)
    wuk = mla_w_uk[0].reshape(MLA_KV_RANK, MLA_HEADS * MLA_NOPE).astype(BF16)
    wuv = mla_w_uv[0].reshape(MLA_KV_RANK, MLA_HEADS * MLA_V).astype(BF16)
    wq_peer = peer_w_q[0].reshape(D, PEER_HEADS * PEER_KEY_DIM).astype(BF16)
    sk = peer_sub_keys[0].reshape(PEER_HEADS * 2, PEER_N_KEYS, PEER_KEY_DIM // 2)
    zk = jnp.zeros_like(sk)
    first = (jnp.arange(PEER_HEADS * 2) % 2 == 0)[:, None, None]
    keys_pad = jnp.where(first, jnp.concatenate([sk, zk], axis=2), jnp.concatenate([zk, sk], axis=2)).astype(BF16)

    tab_m = _rope_tables(pos, MLA_ROPE, LANES)
    tab_d = _rope_tables(pos, DSA_ROT, DSA_HEAD_DIM)
    tab_i = _rope_tables(pos, IDX_ROT, IDX_DIM)

    proj = _inproj(xf, norm_mix_g[0][None], w_pack)
    qm, km, vm, dq, dk, dv, iq, ka, kb = _prep(proj, tab_m, tab_d, tab_i, mla_q_norm_g[0][None],
                                               mla_kv_norm_g[0][None], wuq, wuk, wuv)
    o_a = _mla_attn(qm, km, vm, B, S)
    o_b = _dsa(iq, proj, dq, ka, kb, dk, dv, B, S)
    merged = _merge(o_a, o_b, proj, w_branch_a[0].astype(BF16), w_branch_b[0].astype(BF16))
    x1, h2, qp = _outproj(xf, merged, w_out[0].astype(BF16), norm_ffn_g[0][None], wq_peer)
    idx_t, g_t = _route(qp, keys_pad)
    nk = PEER_HEADS * PEER_TOPK
    idx = idx_t.transpose(0, 2, 1).reshape(T, nk)
    g = g_t.transpose(0, 2, 1).reshape(T, nk)
    half = lambda t: lax.bitcast_convert_type(t.astype(BF16), jnp.uint16).astype(jnp.uint32)
    uv = lax.bitcast_convert_type(half(peer_u[0]) | (half(peer_v[0]) << 16), I32)
    y = _peer(idx, g, h2, x1, norm_final_g[None], uv)
    return y.reshape(B, S, D)
```

```python
import functools

import numpy as np
import jax
import jax.numpy as jnp
from jax import lax
from jax.experimental import pallas as pl
from jax.experimental.pallas import tpu as pltpu

F32 = jnp.float32
BF16 = jnp.bfloat16
I32 = jnp.int32

D_MODEL = 2048
CHUNK = 64
Q_BLOCK = 128
ROPE_THETA = 500000.0
EPS = 1e-6
NEG_INF = -1e30

MLA_HEADS = 8
MLA_NOPE = 128
MLA_ROPE = 64
MLA_V = 128
MLA_Q_RANK = 512
MLA_KV_RANK = 256
MLA_QK_PAD = 256

DSA_HEADS = 8
DSA_KV_HEADS = 2
DSA_GROUP = DSA_HEADS // DSA_KV_HEADS
DSA_HEAD_DIM = 128
DSA_ROT = DSA_HEAD_DIM // 4
IDX_HEADS = 16
IDX_DIM = 64
IDX_ROT = IDX_DIM // 4
DSA_TOPK_MAX = 256

PEER_HEADS = 8
PEER_N_KEYS = 128
PEER_KEY_DIM = 128
PEER_TOPK = 16

IN_SIZES = (MLA_Q_RANK, MLA_KV_RANK, MLA_ROPE, DSA_HEADS * DSA_HEAD_DIM, DSA_KV_HEADS * DSA_HEAD_DIM,
            DSA_KV_HEADS * DSA_HEAD_DIM, IDX_HEADS * IDX_DIM, IDX_DIM, IDX_HEADS, 2 * D_MODEL)

LANES = 128

OFF_GATES = 0
OFF_DQ = 4096
OFF_IQ = 5120
OFF_CQ = 6144
OFF_CKV = 6656
OFF_DK = 6912
OFF_DV = 7168
OFF_KROPE = 7424
OFF_IKW = 7552
N_PACK = 7680

VMEM_LIMIT = 56 * 1024 * 1024


def _params(*sem):
    return pltpu.CompilerParams(dimension_semantics=sem, vmem_limit_bytes=VMEM_LIMIT)


def _rms(t, g):
    return t * lax.rsqrt(jnp.mean(t * t, axis=-1, keepdims=True) + EPS) * g


def _dot(a, b):
    return jnp.dot(a, b, preferred_element_type=F32)


def _dot_nt(a, b):
    return lax.dot_general(a, b, (((1,), (1,)), ((), ())), preferred_element_type=F32)


def _rope(t, c, sa, sb, half):
    return t * c + pltpu.roll(t, half, 1) * sa + pltpu.roll(t, LANES - half, 1) * sb


def _inproj_kernel(x_ref, g_ref, w_ref, o_ref, h_ref):
    @pl.when(pl.program_id(1) == 0)
    def _():
        h_ref[...] = _rms(x_ref[...], g_ref[...]).astype(BF16)

    o_ref[...] = _dot(h_ref[...], w_ref[...])


def _inproj(x, g, w, tm=512, tn=1920):
    T = x.shape[0]
    return pl.pallas_call(
        _inproj_kernel,
        grid=(T // tm, N_PACK // tn),
        in_specs=[pl.BlockSpec((tm, D_MODEL), lambda i, j: (i, 0)),
                  pl.BlockSpec((1, D_MODEL), lambda i, j: (0, 0)),
                  pl.BlockSpec((D_MODEL, tn), lambda i, j: (0, j))],
        out_specs=pl.BlockSpec((tm, tn), lambda i, j: (i, j)),
        out_shape=jax.ShapeDtypeStruct((T, N_PACK), F32),
        scratch_shapes=[pltpu.VMEM((tm, D_MODEL), BF16)],
        compiler_params=_params("parallel", "arbitrary"),
        name="inproj",
    )(x, g, w)


def _prep_kernel(cq_ref, ckv_ref, kr_ref, dq_ref, dk_ref, dv_ref, iq_ref, ikw_ref,
                 tm_ref, td_ref, ti_ref, gq_ref, gkv_ref, wuq_ref, wuk_ref, wuv_ref,
                 qm_ref, km_ref, vm_ref, dqo_ref, dko_ref, dvo_ref, iqo_ref, ka_ref, kb_ref):
    cm, sam, sbm = tm_ref[0], tm_ref[1], tm_ref[2]
    cd, sad, sbd = td_ref[0], td_ref[1], td_ref[2]
    ci, sai, sbi = ti_ref[0], ti_ref[1], ti_ref[2]

    qn = _rms(cq_ref[...], gq_ref[...]).astype(BF16)
    q = _dot(qn, wuq_ref[...])
    for h in range(MLA_HEADS):
        lo = h * MLA_QK_PAD
        qm_ref[:, lo:lo + LANES] = q[:, lo:lo + LANES].astype(BF16)
        qm_ref[:, lo + LANES:lo + 2 * LANES] = _rope(q[:, lo + LANES:lo + 2 * LANES], cm, sam, sbm,
                                                       MLA_ROPE // 2).astype(BF16)
    ckv = _rms(ckv_ref[...], gkv_ref[...]).astype(BF16)
    kn = _dot(ckv, wuk_ref[...])
    kpe = _rope(kr_ref[...], cm, sam, sbm, MLA_ROPE // 2).astype(BF16)
    for h in range(MLA_HEADS):
        lo = h * MLA_QK_PAD
        km_ref[:, lo:lo + LANES] = kn[:, h * MLA_NOPE:(h + 1) * MLA_NOPE].astype(BF16)
        km_ref[:, lo + LANES:lo + 2 * LANES] = kpe
    vm_ref[...] = _dot(ckv, wuv_ref[...]).astype(BF16)

    for h in range(DSA_HEADS):
        sl = slice(h * LANES, (h + 1) * LANES)
        dqo_ref[:, sl] = _rope(dq_ref[:, sl], cd, sad, sbd, DSA_ROT // 2).astype(BF16)
    for h in range(DSA_KV_HEADS):
        sl = slice(h * LANES, (h + 1) * LANES)
        dko_ref[:, sl] = _rope(dk_ref[:, sl], cd, sad, sbd, DSA_ROT // 2).astype(BF16)
    dvo_ref[...] = dv_ref[...].astype(BF16)

    for c in range(IDX_HEADS * IDX_DIM // LANES):
        sl = slice(c * LANES, (c + 1) * LANES)
        iqo_ref[:, sl] = _rope(iq_ref[:, sl], ci, sai, sbi, IDX_ROT // 2).astype(BF16)
    lane = lax.broadcasted_iota(I32, ikw_ref.shape, 1)
    ka = jnp.where(lane < IDX_DIM, _rope(ikw_ref[...], ci, sai, sbi, IDX_ROT // 2), 0.0)
    ka_ref[...] = ka.astype(BF16)
    kb_ref[...] = pltpu.roll(ka, IDX_DIM, 1).astype(BF16)


def _prep(proj, tab_m, tab_d, tab_i, gq, gkv, wuq, wuk, wuv, tm=256):
    T = proj.shape[0]

    def col(width, off):
        return pl.BlockSpec((tm, width), lambda i: (i, off // width))

    def full(a):
        return pl.BlockSpec(a.shape, lambda i: (0,) * a.ndim)

    tab = pl.BlockSpec((3, tm, LANES), lambda i: (0, i, 0))
    widths = (MLA_HEADS * MLA_QK_PAD, MLA_HEADS * MLA_QK_PAD, MLA_HEADS * MLA_V, DSA_HEADS * DSA_HEAD_DIM,
              DSA_KV_HEADS * DSA_HEAD_DIM, DSA_KV_HEADS * DSA_HEAD_DIM, IDX_HEADS * IDX_DIM, LANES, LANES)
    return pl.pallas_call(
        _prep_kernel,
        grid=(T // tm,),
        in_specs=[col(MLA_Q_RANK, OFF_CQ), col(MLA_KV_RANK, OFF_CKV), col(LANES, OFF_KROPE),
                  col(1024, OFF_DQ), col(256, OFF_DK), col(256, OFF_DV), col(1024, OFF_IQ), col(LANES, OFF_IKW),
                  tab, tab, tab, full(gq), full(gkv), full(wuq), full(wuk), full(wuv)],
        out_specs=[pl.BlockSpec((tm, w), lambda i: (i, 0)) for w in widths],
        out_shape=[jax.ShapeDtypeStruct((T, w), BF16) for w in widths],
        compiler_params=_params("parallel"),
        name="prep",
    )(proj, proj, proj, proj, proj, proj, proj, proj, tab_m, tab_d, tab_i, gq, gkv, wuq, wuk, wuv)


def _mla_attn_tile(q_ref, k_ref, v_ref, o_ref, c, tq, scale):
    n = (c + 1) * tq
    q_chunk = (c * tq + lax.broadcasted_iota(I32, (tq, n), 0)) // CHUNK
    k_chunk = lax.broadcasted_iota(I32, (tq, n), 1) // CHUNK
    allowed = k_chunk <= q_chunk
    for h in range(MLA_HEADS):
        qk = slice(h * MLA_QK_PAD, (h + 1) * MLA_QK_PAD)
        s = _dot_nt(q_ref[:, qk], k_ref[0:n, qk]) * scale
        s = jnp.where(allowed, s, NEG_INF)
        p = jnp.exp(s - jnp.max(s, axis=-1, keepdims=True))
        l = jnp.sum(p, axis=-1, keepdims=True)
        vs = slice(h * MLA_V, (h + 1) * MLA_V)
        o_ref[:, vs] = (_dot(p.astype(BF16), v_ref[0:n, vs]) / l).astype(BF16)


def _mla_attn_kernel(q_ref, k_ref, v_ref, o_ref, *, tq, scale):
    for c in range(k_ref.shape[0] // tq):
        pl.when(pl.program_id(1) == c)(functools.partial(_mla_attn_tile, q_ref, k_ref, v_ref, o_ref, c, tq, scale))


def _mla_attn(q, k, v, B, S, tq=256):
    nq = S // tq
    return pl.pallas_call(
        functools.partial(_mla_attn_kernel, tq=tq, scale=(MLA_NOPE + MLA_ROPE) ** -0.5),
        grid=(B, nq),
        in_specs=[pl.BlockSpec((tq, q.shape[1]), lambda b, i: (b * nq + i, 0)),
                  pl.BlockSpec((S, k.shape[1]), lambda b, i: (b, 0)),
                  pl.BlockSpec((S, v.shape[1]), lambda b, i: (b, 0))],
        out_specs=pl.BlockSpec((tq, v.shape[1]), lambda b, i: (b * nq + i, 0)),
        out_shape=jax.ShapeDtypeStruct((B * S, v.shape[1]), BF16),
        compiler_params=_params("parallel", "arbitrary"),
        name="mla_attn",
    )(q, k, v)


def _order_key(score):
    bits = pltpu.bitcast(score + 0.0, I32)
    return bits ^ ((bits >> 31) & jnp.int32(0x7FFFFFFF))


def _dsa_body(iq_ref, ikw_ref, dq_ref, ka_ref, kb_ref, dk_ref, dv_ref, o_ref, S, top_k, scale):
    tq = Q_BLOCK
    q0 = pl.program_id(1) * tq
    q_chunk = (q0 + lax.broadcasted_iota(I32, (tq, S), 0)) // CHUNK
    k_chunk = lax.broadcasted_iota(I32, (tq, S), 1) // CHUNK
    allowed = k_chunk <= q_chunk

    w = ikw_ref[...] * (IDX_HEADS ** -0.5 * IDX_DIM ** -0.5)
    score = jnp.zeros((tq, S), F32)
    for c in range(IDX_HEADS // 2):
        qpair = iq_ref[:, c * LANES:(c + 1) * LANES]
        for half, k_ref in enumerate((ka_ref, kb_ref)):
            h = 2 * c + half
            score = score + w[:, IDX_DIM + h:IDX_DIM + h + 1] * jnp.maximum(_dot_nt(qpair, k_ref[0:S, :]), 0.0)
    score = jnp.where(allowed, score, NEG_INF)

    keys = _order_key(score)
    kf = jnp.float32(top_k)

    def count_ge(cand):
        return jnp.sum((keys >= cand).astype(F32), axis=-1, keepdims=True)

    int_min = jnp.full((tq, 1), jnp.iinfo(jnp.int32).min, I32)
    thr = jnp.where(count_ge(jnp.zeros((tq, 1), I32)) >= kf, 0, int_min)

    def step(i, thr):
        cand = thr + jnp.left_shift(jnp.int32(1), 30 - i)
        return jnp.where(count_ge(cand) >= kf, cand, thr)

    thr = lax.fori_loop(0, 31, step, thr)

    gt = keys > thr
    eq = keys == thr
    need = kf - jnp.sum(gt.astype(F32), axis=-1, keepdims=True)
    tri = (lax.broadcasted_iota(I32, (LANES, LANES), 0) <= lax.broadcasted_iota(I32, (LANES, LANES), 1)).astype(BF16)
    run = jnp.zeros((tq, 1), F32)
    sel = []
    for c in range(S // LANES):
        sl = slice(c * LANES, (c + 1) * LANES)
        prefix = _dot(eq[:, sl].astype(BF16), tri) + run
        sel.append(gt[:, sl] | (eq[:, sl] & (prefix <= need)))
        run = prefix[:, LANES - 1:LANES]
    mask = jnp.concatenate(sel, axis=1) & allowed

    mask_g = jnp.concatenate([mask] * DSA_GROUP, axis=0)
    for g in range(DSA_KV_HEADS):
        kv = slice(g * DSA_HEAD_DIM, (g + 1) * DSA_HEAD_DIM)
        qg = jnp.concatenate([dq_ref[:, (g * DSA_GROUP + n) * DSA_HEAD_DIM:(g * DSA_GROUP + n + 1) * DSA_HEAD_DIM]
                              for n in range(DSA_GROUP)], axis=0)
        s = _dot_nt(qg, dk_ref[0:S, kv]) * scale
        s = jnp.where(mask_g, s, NEG_INF)
        p = jnp.exp(s - jnp.max(s, axis=-1, keepdims=True))
        l = jnp.sum(p, axis=-1, keepdims=True)
        o = _dot(p.astype(BF16), dv_ref[0:S, kv]) / l
        for n in range(DSA_GROUP):
            hs = (g * DSA_GROUP + n) * DSA_HEAD_DIM
            o_ref[:, hs:hs + DSA_HEAD_DIM] = o[n * tq:(n + 1) * tq].astype(BF16)


DSA_EXTENTS = 8


def _dsa_kernel(*refs, top_k, scale):
    S = refs[3].shape[0]
    nq = S // Q_BLOCK
    nvar = DSA_EXTENTS if nq % DSA_EXTENTS == 0 and S // DSA_EXTENTS >= top_k else 1
    per = nq // nvar
    for v in range(nvar):
        pl.when(pl.program_id(1) // per == v)(
            functools.partial(_dsa_body, *refs, (v + 1) * per * Q_BLOCK, top_k, scale))


def _dsa(iq, proj, dq, ka, kb, dk, dv, B, S):
    tq = Q_BLOCK
    nq = S // tq

    def qblk(width, off=0):
        return pl.BlockSpec((tq, width), lambda b, i: (b * nq + i, off // width))

    def kblk(width):
        return pl.BlockSpec((S, width), lambda b, i: (b, 0))

    return pl.pallas_call(
        functools.partial(_dsa_kernel, top_k=min(DSA_TOPK_MAX, S // 4), scale=DSA_HEAD_DIM ** -0.5),
        grid=(B, nq),
        in_specs=[qblk(1024), qblk(LANES, OFF_IKW), qblk(1024), kblk(LANES), kblk(LANES), kblk(256), kblk(256)],
        out_specs=qblk(1024),
        out_shape=jax.ShapeDtypeStruct((B * S, DSA_HEADS * DSA_HEAD_DIM), BF16),
        compiler_params=_params("parallel", "arbitrary"),
        name="dsa",
    )(iq, proj, dq, ka, kb, dk, dv)


def _merge_kernel(oa_ref, ob_ref, ga_ref, gb_ref, wa_ref, wb_ref, o_ref):
    a = _dot(oa_ref[...], wa_ref[...])
    b = _dot(ob_ref[...], wb_ref[...])
    o_ref[...] = (jax.nn.sigmoid(ga_ref[...]) * a + jax.nn.sigmoid(gb_ref[...]) * b).astype(BF16)


def _merge(oa, ob, proj, wa, wb, tm=256):
    T = oa.shape[0]
    row = lambda w, c=0: pl.BlockSpec((tm, w), lambda i: (i, c))
    full = lambda a: pl.BlockSpec(a.shape, lambda i: (0, 0))
    return pl.pallas_call(
        _merge_kernel,
        grid=(T // tm,),
        in_specs=[row(1024), row(1024), row(D_MODEL, 0), row(D_MODEL, 1), full(wa), full(wb)],
        out_specs=row(D_MODEL),
        out_shape=jax.ShapeDtypeStruct((T, D_MODEL), BF16),
        compiler_params=_params("parallel"),
        name="merge",
    )(oa, ob, proj, proj, wa, wb)


def _outproj_kernel(x_ref, m_ref, wo_ref, g_ref, wq_ref, x1_ref, h2_ref, qp_ref):
    x1 = x_ref[...] + _dot(m_ref[...], wo_ref[...])
    x1_ref[...] = x1
    h2 = _rms(x1, g_ref[...]).astype(BF16)
    h2_ref[...] = h2
    qp_ref[...] = _dot(h2, wq_ref[...]).astype(BF16)


def _outproj(x, merged, wo, g, wq, tm=256):
    T = x.shape[0]
    row = lambda w: pl.BlockSpec((tm, w), lambda i: (i, 0))
    full = lambda a: pl.BlockSpec(a.shape, lambda i: (0, 0))
    nq = wq.shape[1]
    return pl.pallas_call(
        _outproj_kernel,
        grid=(T // tm,),
        in_specs=[row(D_MODEL), row(D_MODEL), full(wo), full(g), full(wq)],
        out_specs=[row(D_MODEL), row(D_MODEL), row(nq)],
        out_shape=[jax.ShapeDtypeStruct((T, D_MODEL), F32), jax.ShapeDtypeStruct((T, D_MODEL), BF16),
                   jax.ShapeDtypeStruct((T, nq), BF16)],
        compiler_params=_params("parallel"),
        name="outproj",
    )(x, merged, wo, g, wq)


def _top16(cur, pos=None, payload=None):
    if pos is None:
        pos = lax.broadcasted_iota(I32, cur.shape, 0).astype(F32)
    vals, idxs = [], []
    for _ in range(PEER_TOPK):
        m = jnp.max(cur, axis=0, keepdims=True)
        idx = jnp.min(jnp.where(cur == m, pos, jnp.inf), axis=0, keepdims=True)
        hit = pos == idx
        vals.append(m)
        if payload is None:
            idxs.append(idx)
        else:
            idxs.append(jnp.max(jnp.where(hit, payload, -1.0), axis=0, keepdims=True))
        cur = jnp.where(hit, -jnp.inf, cur)
    return jnp.concatenate(vals, axis=0), jnp.concatenate(idxs, axis=0)


def _pair_candidates(s1, i1, s2, i2):
    tb = s1.shape[1]
    sub = lax.broadcasted_iota(I32, (8, tb), 0).astype(F32)
    cs, cp, ci = ([s1[0:1] + s2], [lax.broadcasted_iota(I32, (PEER_TOPK, tb), 0).astype(F32)],
                  [i1[0:1] * PEER_N_KEYS + i2])
    for a in range(1, 8):
        keep = sub < PEER_TOPK // (a + 1)
        cs.append(jnp.where(keep, s1[a:a + 1] + s2[0:8], -jnp.inf))
        cp.append(a * PEER_TOPK + sub)
        ci.append(i1[a:a + 1] * PEER_N_KEYS + i2[0:8])
    cs.append(s1[8:16] + s2[0:1])
    cp.append((8 + sub) * PEER_TOPK)
    ci.append(i1[8:16] * PEER_N_KEYS + i2[0:1])
    return jnp.concatenate(cs, axis=0), jnp.concatenate(cp, axis=0), jnp.concatenate(ci, axis=0)


def _route_kernel(qp_ref, keys_ref, idx_ref, g_ref):
    for h in range(PEER_HEADS):
        qpair = qp_ref[:, h * PEER_KEY_DIM:(h + 1) * PEER_KEY_DIM]
        s1, i1 = _top16(_dot_nt(keys_ref[2 * h], qpair))
        s2, i2 = _top16(_dot_nt(keys_ref[2 * h + 1], qpair))
        cand_s, cand_p, cand_i = _pair_candidates(s1, i1, s2, i2)
        top_s, expert = _top16(cand_s, cand_p, cand_i)
        e = jnp.exp(top_s - top_s[0:1])
        g = e / jnp.sum(e, axis=0, keepdims=True)
        idx_ref[0, h * PEER_TOPK:(h + 1) * PEER_TOPK, :] = expert.astype(I32)
        g_ref[0, h * PEER_TOPK:(h + 1) * PEER_TOPK, :] = g


def _route(qp, keys_pad, tb=128):
    T = qp.shape[0]
    nb = T // tb
    nk = PEER_HEADS * PEER_TOPK
    out = pl.BlockSpec((1, nk, tb), lambda i: (i, 0, 0))
    return pl.pallas_call(
        _route_kernel,
        grid=(nb,),
        in_specs=[pl.BlockSpec((tb, qp.shape[1]), lambda i: (i, 0)),
                  pl.BlockSpec(keys_pad.shape, lambda i: (0, 0, 0))],
        out_specs=[out, out],
        out_shape=[jax.ShapeDtypeStruct((nb, nk, tb), I32), jax.ShapeDtypeStruct((nb, nk, tb), F32)],
        compiler_params=_params("parallel"),
        name="peer_route",
    )(qp, keys_pad)


def _pack_kernel(u_ref, v_ref, o_ref):
    lo = pltpu.bitcast(u_ref[...].astype(BF16).astype(F32), I32)
    hi = pltpu.bitcast(v_ref[...].astype(BF16).astype(F32), I32)
    w = ((lo >> 16) & jnp.int32(0xFFFF)) | (hi & jnp.int32(-65536))
    rows = w.shape[0]
    for c in range(D_MODEL // LANES):
        o_ref[:, c, :, :] = w[:, c * LANES:(c + 1) * LANES].reshape(rows, 1, LANES)


def _pack_tables(u, v, tr=256):
    E = u.shape[0]
    row = pl.BlockSpec((tr, D_MODEL), lambda i: (i, 0))
    return pl.pallas_call(
        _pack_kernel,
        grid=(E // tr,),
        in_specs=[row, row],
        out_specs=pl.BlockSpec((tr, D_MODEL // LANES, 1, LANES), lambda i: (i, 0, 0, 0)),
        out_shape=jax.ShapeDtypeStruct((E, D_MODEL // LANES, 1, LANES), I32),
        compiler_params=_params("parallel"),
        name="peer_pack",
    )(u, v)


PEER_TG = 8
PEER_EARLY = 4


def _peer_issue(idx_ref, t0, uv_hbm, buf, sem, first=0, last=PEER_TG):
    nk = PEER_HEADS * PEER_TOPK
    for t in range(first, last):
        for k in range(nk):
            r = t * nk + k
            pltpu.async_copy(uv_hbm.at[idx_ref[t0 + t, k]], buf.at[r // 8, :, pl.ds(r % 8, 1), :], sem,
                             priority=k % 2)


def _peer_wait(other, buf, sem):
    pltpu.make_async_copy(other, buf, sem).wait()


def _peer_compute(buf, g, h, x1, gf):
    nk = PEER_HEADS * PEER_TOPK
    rows = PEER_TG * nk
    w = jnp.concatenate([buf[:, c].reshape(rows, LANES) for c in range(D_MODEL // LANES)], axis=1)
    ub = pltpu.bitcast(w << 16, F32).astype(BF16)
    vb = pltpu.bitcast(w & jnp.int32(-65536), F32).astype(BF16)
    res = _dot_nt(h, ub)
    row = lax.broadcasted_iota(I32, (PEER_TG, rows), 0)
    diag = (lax.broadcasted_iota(I32, (PEER_TG, rows), 1) // nk) == row
    res = jnp.where(diag, res, 0.0)
    a = res[:, 0:nk]
    for j in range(1, PEER_TG):
        a = a + res[:, j * nk:(j + 1) * nk]
    act = 0.5 * a * (1.0 + lax.erf(a * np.float32(2.0 ** -0.5)))
    coef = g * act
    coef_bd = jnp.where(diag, jnp.concatenate([coef] * PEER_TG, axis=1), 0.0).astype(BF16)
    out = _dot(coef_bd, vb)
    return _rms(x1 + out, gf)


def _peer_kernel(idx_ref, nxt_ref, g_ref, h_ref, x1_ref, gf_ref, uv_hbm, y_ref, buf_a, buf_b, sem):
    i = pl.program_id(0)
    lo, hi = slice(0, PEER_TG), slice(PEER_TG, 2 * PEER_TG)

    @pl.when(i == 0)
    def _():
        _peer_issue(idx_ref, 0, uv_hbm, buf_a, sem.at[0])

    _peer_issue(idx_ref, PEER_TG, uv_hbm, buf_b, sem.at[1], 0, PEER_EARLY)
    _peer_wait(buf_b, buf_a, sem.at[0])
    _peer_issue(idx_ref, PEER_TG, uv_hbm, buf_b, sem.at[1], PEER_EARLY, PEER_TG)
    y_ref[lo, :] = _peer_compute(buf_a, g_ref[lo, :], h_ref[lo, :], x1_ref[lo, :], gf_ref[...])
    _peer_issue(nxt_ref, 0, uv_hbm, buf_a, sem.at[0], 0, PEER_EARLY)
    _peer_wait(buf_a, buf_b, sem.at[1])
    _peer_issue(nxt_ref, 0, uv_hbm, buf_a, sem.at[0], PEER_EARLY, PEER_TG)
    y_ref[hi, :] = _peer_compute(buf_b, g_ref[hi, :], h_ref[hi, :], x1_ref[hi, :], gf_ref[...])

    @pl.when(i == pl.num_programs(0) - 1)
    def _():
        _peer_wait(buf_b, buf_a, sem.at[0])


def _peer(idx, g, h2, x1, gf, uv):
    T = idx.shape[0]
    nk = idx.shape[1]
    ts = 2 * PEER_TG
    n = T // ts
    row = lambda w: pl.BlockSpec((ts, w), lambda i: (i, 0))
    return pl.pallas_call(
        _peer_kernel,
        grid=(n,),
        in_specs=[pl.BlockSpec((ts, nk), lambda i: (i, 0), memory_space=pltpu.SMEM),
                  pl.BlockSpec((ts, nk), lambda i: (jnp.minimum(i + 1, n - 1), 0), memory_space=pltpu.SMEM),
                  row(nk), row(D_MODEL), row(D_MODEL), pl.BlockSpec(gf.shape, lambda i: (0, 0)),
                  pl.BlockSpec(memory_space=pl.ANY)],
        out_specs=row(D_MODEL),
        out_shape=jax.ShapeDtypeStruct((T, D_MODEL), F32),
        scratch_shapes=[pltpu.VMEM((PEER_TG * nk // 8, D_MODEL // LANES, 8, LANES), I32),
                        pltpu.VMEM((PEER_TG * nk // 8, D_MODEL // LANES, 8, LANES), I32),
                        pltpu.SemaphoreType.DMA((2,))],
        compiler_params=_params("arbitrary"),
        name="peer_ffn",
    )(idx, idx, g, h2, x1, gf, uv)


def _rope_tables(pos, rot, width):
    half = rot // 2
    inv_freq = ROPE_THETA ** (-(jnp.arange(half, dtype=F32) * 2.0) / rot)
    ang = pos.astype(F32)[:, None] * inv_freq
    cos, sin = jnp.cos(ang), jnp.sin(ang)
    T = pos.shape[0]
    one = jnp.ones((T, width - rot), F32)
    zero = jnp.zeros((T, width - rot), F32)
    zh = jnp.zeros((T, half), F32)
    c = jnp.concatenate([cos, cos, one], axis=1)
    sa = jnp.concatenate([zh, sin, zero], axis=1)
    sb = jnp.concatenate([-sin, zh, zero], axis=1)
    reps = LANES // width
    return jnp.stack([jnp.tile(t, (1, reps)) for t in (c, sa, sb)])


def kernel(x, positions, norm_mix_g, w_in, mla_q_norm_g, mla_kv_norm_g, mla_w_uq, mla_w_uk, mla_w_uv, w_branch_a, w_branch_b, w_out, norm_ffn_g, peer_w_q, peer_sub_keys, peer_u, peer_v, norm_final_g):
    B, S, D = x.shape
    T = B * S
    assert D == D_MODEL and w_in.shape[0] == 1 and S % 256 == 0 and T % 512 == 0
    xf = x.reshape(T, D)
    pos = positions.reshape(T)

    splits = [int(c) for c in np.cumsum(IN_SIZES)[:-1]]
    w_cq, w_ckv, w_kr, w_dq, w_dk, w_dv, w_iq, w_ik, w_iw, w_gates = jnp.split(w_in[0], splits, axis=1)
    zc = lambda n: jnp.zeros((D, n), F32)
    w_pack = jnp.concatenate([w_gates, w_dq, w_iq, w_cq, w_ckv, w_dk, w_dv, w_kr, zc(64), w_ik, w_iw, zc(48)],
                             axis=1).astype(BF16)
    wuq = jnp.pad(mla_w_uq[0], ((0, 0), (0, 0), (0, MLA_QK_PAD - MLA_NOPE - MLA_ROPE)))
    wuq = wuq.reshape(MLA_Q_RANK, MLA_HEADS * MLA_QK_PAD).astype(BF16)
    wuk = mla_w_uk[0].reshape(MLA_KV_RANK, MLA_HEADS * MLA_NOPE).astype(BF16)
    wuv = mla_w_uv[0].reshape(MLA_KV_RANK, MLA_HEADS * MLA_V).astype(BF16)
    wq_peer = peer_w_q[0].reshape(D, PEER_HEADS * PEER_KEY_DIM).astype(BF16)
    sk = peer_sub_keys[0].reshape(PEER_HEADS * 2, PEER_N_KEYS, PEER_KEY_DIM // 2)
    zk = jnp.zeros_like(sk)
    first = (jnp.arange(PEER_HEADS * 2) % 2 == 0)[:, None, None]
    keys_pad = jnp.where(first, jnp.concatenate([sk, zk], axis=2), jnp.concatenate([zk, sk], axis=2)).astype(BF16)

    tab_m = _rope_tables(pos, MLA_ROPE, LANES)
    tab_d = _rope_tables(pos, DSA_ROT, DSA_HEAD_DIM)
    tab_i = _rope_tables(pos, IDX_ROT, IDX_DIM)

    proj = _inproj(xf, norm_mix_g[0][None], w_pack)
    qm, km, vm, dq, dk, dv, iq, ka, kb = _prep(proj, tab_m, tab_d, tab_i, mla_q_norm_g[0][None],
                                               mla_kv_norm_g[0][None], wuq, wuk, wuv)
    o_a = _mla_attn(qm, km, vm, B, S)
    o_b = _dsa(iq, proj, dq, ka, kb, dk, dv, B, S)
    merged = _merge(o_a, o_b, proj, w_branch_a[0].astype(BF16), w_branch_b[0].astype(BF16))
    x1, h2, qp = _outproj(xf, merged, w_out[0].astype(BF16), norm_ffn_g[0][None], wq_peer)
    idx_t, g_t = _route(qp, keys_pad)
    nk = PEER_HEADS * PEER_TOPK
    idx = idx_t.transpose(0, 2, 1).reshape(T, nk)
    g = g_t.transpose(0, 2, 1).reshape(T, nk)
    uv = _pack_tables(peer_u[0], peer_v[0])
    y = _peer(idx, g, h2, x1, norm_final_g[None], uv)
    return y.reshape(B, S, D)
```
